```python
import math
import jax, jax.numpy as jnp
from jax import lax
import numpy as np

D_MODEL = 1024
BATCH = 32
SEQ = 2048
DEPTH = 4

N_MIXERS = 2
N_MLA_LAYERS = (DEPTH + 1) // 2
N_SWA_LAYERS = DEPTH // 2
N_DENSE_LAYERS = (DEPTH + 1) // 2
N_MOE_LAYERS = DEPTH // 2

MLA_HEADS = 8
MLA_Q_RANK = 256
MLA_KV_RANK = 256
MLA_NOPE_DIM = 128
MLA_ROPE_DIM = 64
MLA_V_DIM = 128
MLA_QK_DIM = MLA_NOPE_DIM + MLA_ROPE_DIM
MLA_DOWN_DIM = MLA_Q_RANK + MLA_KV_RANK + MLA_ROPE_DIM
ROPE_THETA = 10000.0

SWA_Q_HEADS = 16
SWA_KV_HEADS = 4
SWA_GROUP = SWA_Q_HEADS // SWA_KV_HEADS
SWA_HEAD_DIM = 64
SWA_WINDOW = 128
SWA_Q_DIM = SWA_Q_HEADS * SWA_HEAD_DIM
SWA_KV_DIM = SWA_KV_HEADS * SWA_HEAD_DIM
SWA_QKV_DIM = SWA_Q_DIM + 2 * SWA_KV_DIM

BLOCK = 128

FFN_DIM = 2816
N_EXPERTS = 8
TOP_K = 2
EXPERT_DIM = 2048

EPS = 1e-6
NEG_INF = -1e30

kernel_name = "hybrid_mla_swa_moe_encoder"


def rmsnorm(x, g):
    xf = x.astype(jnp.float32)
    y = xf * lax.rsqrt(jnp.mean(xf * xf, axis=-1, keepdims=True) + EPS)
    return (y * g.astype(jnp.float32)).astype(x.dtype)


def rope_tables(seq, dim):
    inv = 1.0 / (ROPE_THETA ** (jnp.arange(0, dim, 2, dtype=jnp.float32) / dim))
    ang = jnp.arange(seq, dtype=jnp.float32)[:, None] * inv[None, :]
    return jnp.cos(ang), jnp.sin(ang)


def apply_rope(x, cos, sin):
    xf = x.astype(jnp.float32)
    x1, x2 = jnp.split(xf, 2, axis=-1)
    return jnp.concatenate([x1 * cos - x2 * sin, x2 * cos + x1 * sin], axis=-1).astype(x.dtype)


def alibi_slopes(n):
    return 2.0 ** (-8.0 * jnp.arange(1, n + 1, dtype=jnp.float32) / n)


def mla(h, w_dqkv, q_norm, w_uq, kv_norm, w_uk, w_uv, w_o):
    B, S, _ = h.shape
    nb = S // BLOCK
    cos, sin = rope_tables(S, MLA_ROPE_DIM)
    down = h @ w_dqkv
    c_q = rmsnorm(down[..., :MLA_Q_RANK], q_norm)
    c_kv = rmsnorm(down[..., MLA_Q_RANK:MLA_Q_RANK + MLA_KV_RANK], kv_norm)
    k_rope = apply_rope(down[..., MLA_Q_RANK + MLA_KV_RANK:], cos, sin)
    q = jnp.einsum('bsr,rhd->bshd', c_q, w_uq)
    q_nope = q[..., :MLA_NOPE_DIM]
    q_rope = apply_rope(q[..., MLA_NOPE_DIM:], cos[:, None, :], sin[:, None, :])
    k_nope = jnp.einsum('bsr,rhd->bshd', c_kv, w_uk)
    v = jnp.einsum('bsr,rhd->bshd', c_kv, w_uv)
    scale = MLA_QK_DIM ** -0.5
    qn_b = jnp.moveaxis(q_nope.reshape(B, nb, BLOCK, MLA_HEADS, MLA_NOPE_DIM), 1, 0)
    qr_b = jnp.moveaxis(q_rope.reshape(B, nb, BLOCK, MLA_HEADS, MLA_ROPE_DIM), 1, 0)

    def attend(blk):
        qn, qr = blk
        s = (jnp.einsum('bqhd,bkhd->bhqk', qn, k_nope)
             + jnp.einsum('bqhd,bkd->bhqk', qr, k_rope))
        p = jax.nn.softmax(s.astype(jnp.float32) * scale, axis=-1).astype(v.dtype)
        return jnp.einsum('bhqk,bkhd->bqhd', p, v)

    o = lax.map(attend, (qn_b, qr_b))
    o = jnp.moveaxis(o, 0, 1).reshape(B, S, MLA_HEADS * MLA_V_DIM)
    return o @ w_o


def swa(h, w_qkv, sink, w_o):
    B, S, _ = h.shape
    nb = S // BLOCK
    span = BLOCK + 2 * SWA_WINDOW
    qkv = h @ w_qkv
    q = qkv[..., :SWA_Q_DIM].reshape(B, S, SWA_KV_HEADS, SWA_GROUP, SWA_HEAD_DIM)
    k = qkv[..., SWA_Q_DIM:SWA_Q_DIM + SWA_KV_DIM].reshape(B, S, SWA_KV_HEADS, SWA_HEAD_DIM)
    v = qkv[..., SWA_Q_DIM + SWA_KV_DIM:].reshape(B, S, SWA_KV_HEADS, SWA_HEAD_DIM)
    padw = ((0, 0), (SWA_WINDOW, SWA_WINDOW), (0, 0), (0, 0))
    kp = jnp.pad(k, padw)
    vp = jnp.pad(v, padw)
    slopes = alibi_slopes(SWA_Q_HEADS).reshape(SWA_KV_HEADS, SWA_GROUP)
    sink_l = sink.astype(jnp.float32).reshape(SWA_KV_HEADS, SWA_GROUP)
    scale = SWA_HEAD_DIM ** -0.5

    def block(i):
        start = i * BLOCK
        qb = lax.dynamic_slice_in_dim(q, start, BLOCK, axis=1)
        kb = lax.dynamic_slice_in_dim(kp, start, span, axis=1)
        vb = lax.dynamic_slice_in_dim(vp, start, span, axis=1)
        qpos = start + jnp.arange(BLOCK)
        kpos = start - SWA_WINDOW + jnp.arange(span)
        dist = jnp.abs(qpos[:, None] - kpos[None, :])
        valid = (dist <= SWA_WINDOW) & (kpos >= 0)[None, :] & (kpos < S)[None, :]
        s = jnp.einsum('bqgrd,bkgd->bgrqk', qb, kb).astype(jnp.float32) * scale
        s = s - slopes[:, :, None, None] * dist.astype(jnp.float32)
        s = jnp.where(valid, s, NEG_INF)
        snk = jnp.broadcast_to(sink_l[None, :, :, None, None], s.shape[:-1] + (1,))
        p = jax.nn.softmax(jnp.concatenate([s, snk], axis=-1), axis=-1)[..., :span]
        return jnp.einsum('bgrqk,bkgd->bqgrd', p.astype(vb.dtype), vb)

    o = lax.map(block, jnp.arange(nb))
    o = jnp.moveaxis(o, 0, 1).reshape(B, S, SWA_Q_DIM)
    return o @ w_o


def swiglu(h, w_gate, w_up, w_down):
    return (jax.nn.silu(h @ w_gate) * (h @ w_up)) @ w_down


def moe(h, router, w_gate, w_up, w_down):
    B, S, D = h.shape
    t = h.reshape(-1, D)
    logits = (t @ router).astype(jnp.float32)
    top_val, top_idx = lax.top_k(logits, TOP_K)
    gates = jax.nn.softmax(top_val, axis=-1)
    combine = jnp.sum(jax.nn.one_hot(top_idx, N_EXPERTS, dtype=jnp.float32) * gates[..., None], axis=1)
    combine = combine.astype(t.dtype)
    y = jnp.zeros_like(t)
    for e in range(N_EXPERTS):
        y = y + combine[:, e:e + 1] * swiglu(t, w_gate[e], w_up[e], w_down[e])
    return y.reshape(B, S, D)


def setup_inputs(seed: int = 0) -> dict:
    key = jax.random.key(seed)
    ks = iter(jax.random.split(key, 32))

    def nrm(shape, fan_in):
        return jax.random.normal(next(ks), shape, jnp.float32) * (fan_in ** -0.5)

    def gain(shape):
        return 1.0 + 0.02 * jax.random.normal(next(ks), shape, jnp.float32)

    NA, NS, ND, NM = N_MLA_LAYERS, N_SWA_LAYERS, N_DENSE_LAYERS, N_MOE_LAYERS
    D = D_MODEL
    return {
        "x": jax.random.normal(next(ks), (BATCH, SEQ, D), jnp.float32),
        "mla_norm": gain((NA, D)),
        "mla_w_dqkv": nrm((NA, D, MLA_DOWN_DIM), D),
        "mla_q_norm": gain((NA, MLA_Q_RANK)),
        "mla_w_uq": nrm((NA, MLA_Q_RANK, MLA_HEADS, MLA_QK_DIM), MLA_Q_RANK),
        "mla_kv_norm": gain((NA, MLA_KV_RANK)),
        "mla_w_uk": nrm((NA, MLA_KV_RANK, MLA_HEADS, MLA_NOPE_DIM), MLA_KV_RANK),
        "mla_w_uv": nrm((NA, MLA_KV_RANK, MLA_HEADS, MLA_V_DIM), MLA_KV_RANK),
        "mla_w_o": nrm((NA, MLA_HEADS * MLA_V_DIM, D), MLA_HEADS * MLA_V_DIM),
        "swa_norm": gain((NS, D)),
        "swa_w_qkv": nrm((NS, D, SWA_QKV_DIM), D),
        "swa_sink": 0.5 * jax.random.normal(next(ks), (NS, SWA_Q_HEADS), jnp.float32),
        "swa_w_o": nrm((NS, SWA_Q_DIM, D), SWA_Q_DIM),
        "ffn_norm": gain((ND, D)),
        "ffn_w_gate": nrm((ND, D, FFN_DIM), D),
        "ffn_w_up": nrm((ND, D, FFN_DIM), D),
        "ffn_w_down": nrm((ND, FFN_DIM, D), FFN_DIM),
        "moe_norm": gain((NM, D)),
        "moe_router": nrm((NM, D, N_EXPERTS), D),
        "moe_w_gate": nrm((NM, N_EXPERTS, D, EXPERT_DIM), D),
        "moe_w_up": nrm((NM, N_EXPERTS, D, EXPERT_DIM), D),
        "moe_w_down": nrm((NM, N_EXPERTS, EXPERT_DIM, D), EXPERT_DIM),
        "final_norm": gain((D,)),
    }


def reference(x, mla_norm, mla_w_dqkv, mla_q_norm, mla_w_uq, mla_kv_norm, mla_w_uk, mla_w_uv, mla_w_o,
              swa_norm, swa_w_qkv, swa_sink, swa_w_o,
              ffn_norm, ffn_w_gate, ffn_w_up, ffn_w_down,
              moe_norm, moe_router, moe_w_gate, moe_w_up, moe_w_down,
              final_norm):
    h = x
    for i in range(DEPTH):
        j = i // N_MIXERS
        if i % N_MIXERS == 0:
            h = h + mla(rmsnorm(h, mla_norm[j]), mla_w_dqkv[j], mla_q_norm[j], mla_w_uq[j],
                        mla_kv_norm[j], mla_w_uk[j], mla_w_uv[j], mla_w_o[j])
        else:
            h = h + swa(rmsnorm(h, swa_norm[j]), swa_w_qkv[j], swa_sink[j], swa_w_o[j])
        c = i // 2
        if i % 2 == 0:
            h = h + swiglu(rmsnorm(h, ffn_norm[c]), ffn_w_gate[c], ffn_w_up[c], ffn_w_down[c])
        else:
            h = h + moe(rmsnorm(h, moe_norm[c]), moe_router[c], moe_w_gate[c], moe_w_up[c], moe_w_down[c])
    return rmsnorm(h, final_norm)
```

```python
import functools

import jax
import jax.numpy as jnp
from jax import lax
from jax.experimental import pallas as pl
from jax.experimental.pallas import tpu as pltpu

F32 = jnp.float32
BF16 = jnp.bfloat16
I32 = jnp.int32

D_MODEL = 1024
EPS = 1e-6
NEG_INF = -1e30
ROPE_THETA = 10000.0

MLA_HEADS = 8
MLA_Q_RANK = 256
MLA_KV_RANK = 256
MLA_NOPE = 128
MLA_ROPE = 64
MLA_V = 128
MLA_QK = MLA_NOPE + MLA_ROPE
MLA_HEAD_LANES = 256

SWA_Q_HEADS = 16
SWA_KV_HEADS = 4
SWA_GROUP = 4
SWA_HEAD_DIM = 64
SWA_WINDOW = 128
SWA_BLOCK = 128

FFN_DIM = 2816
N_EXPERTS = 8
EXPERT_DIM = 2048

LANES = 128
VMEM_LIMIT = 48 * 1024 * 1024

ROW_TILE = 512
ATTN_Q_TILE = 256
FFN_TILE = 1408
MOE_CHUNK = 512
MOE_GRANULE = 16
MOE_CHUNK_ROWS = 2 * MOE_CHUNK + N_EXPERTS * MOE_GRANULE
MOE_GRANULES = MOE_CHUNK_ROWS // MOE_GRANULE
MOE_ROW_TILE = 512
MOE_F_TILE = 1024


def _params(*sem):
    return pltpu.CompilerParams(dimension_semantics=sem, vmem_limit_bytes=VMEM_LIMIT)


def _rms(x, g):
    return x * lax.rsqrt(jnp.mean(x * x, axis=-1, keepdims=True) + EPS) * g


def _dot(a, b):
    return jnp.dot(a, b, preferred_element_type=F32)


def _dot_nt(a, b):
    return lax.dot_general(a, b, (((1,), (1,)), ((), ())), preferred_element_type=F32)


def _mla_proj_kernel(h_ref, g_ref, wd_ref, qn_ref, kvn_ref, wuq_ref, wukv_ref, cs_ref,
                     q_ref, k_ref, v_ref):
    xn = _rms(h_ref[...], g_ref[...]).astype(BF16)
    down = _dot(xn, wd_ref[...])
    cq = _rms(down[:, :MLA_Q_RANK], qn_ref[...]).astype(BF16)
    ckv = _rms(down[:, MLA_Q_RANK:MLA_Q_RANK + MLA_KV_RANK], kvn_ref[...]).astype(BF16)
    cs = cs_ref[...]
    lane = lax.broadcasted_iota(I32, cs.shape, 1)

    def rope(a):
        p = a * cs
        return p + pltpu.roll(p, MLA_ROPE, 1)

    kr = jnp.where(lane < MLA_ROPE, rope(down[:, 2 * MLA_Q_RANK:]), 0.0)
    q = _dot(cq, wuq_ref[...]) * (MLA_QK ** -0.5)
    for h in range(MLA_HEADS):
        c = h * MLA_HEAD_LANES
        q_ref[:, c:c + LANES] = q[:, c:c + LANES].astype(BF16)
        q_ref[:, c + LANES:c + 2 * LANES] = rope(q[:, c + LANES:c + 2 * LANES]).astype(BF16)
    kv = _dot(ckv, wukv_ref[...])
    nk = MLA_HEADS * MLA_NOPE
    k_ref[:, :nk] = kv[:, :nk].astype(BF16)
    k_ref[:, nk:] = kr.astype(BF16)
    v_ref[...] = kv[:, nk:].astype(BF16)


def _mla_proj(h, seq, gain, wd, qn, kvn, wuq, wukv, cs):
    n = h.shape[0]
    t = ROW_TILE
    sb = seq // t
    full = lambda a: pl.BlockSpec(a.shape, lambda i: (0, 0))
    return pl.pallas_call(
        _mla_proj_kernel,
        grid=(n // t,),
        in_specs=[pl.BlockSpec((t, D_MODEL), lambda i: (i, 0)), full(gain), full(wd), full(qn), full(kvn),
                  full(wuq), full(wukv), pl.BlockSpec((t, LANES), lambda i: (i % sb, 0))],
        out_specs=[pl.BlockSpec((t, MLA_HEADS * MLA_HEAD_LANES), lambda i: (i, 0)),
                   pl.BlockSpec((t, MLA_HEADS * MLA_NOPE + LANES), lambda i: (i, 0)),
                   pl.BlockSpec((t, MLA_HEADS * MLA_V), lambda i: (i, 0))],
        out_shape=[jax.ShapeDtypeStruct((n, MLA_HEADS * MLA_HEAD_LANES), BF16),
                   jax.ShapeDtypeStruct((n, MLA_HEADS * MLA_NOPE + LANES), BF16),
                   jax.ShapeDtypeStruct((n, MLA_HEADS * MLA_V), BF16)],
        compiler_params=_params("parallel"),
        name="mla_proj",
    )(h, gain, wd, qn, kvn, wuq, wukv, cs)


def _mla_attn_kernel(q_ref, kn_ref, kr_ref, v_ref, o_ref, kcat_ref):
    kcat_ref[:, :LANES] = kn_ref[...]
    kcat_ref[:, LANES:] = kr_ref[...]
    seq = q_ref.shape[0]

    def body(i, carry):
        r0 = pl.multiple_of(i * ATTN_Q_TILE, ATTN_Q_TILE)
        s = _dot_nt(q_ref[pl.ds(r0, ATTN_Q_TILE), :], kcat_ref[...])
        p = jnp.exp(s - jnp.max(s, axis=-1, keepdims=True))
        l = jnp.sum(p, axis=-1, keepdims=True)
        o = _dot(p.astype(BF16), v_ref[...])
        o_ref[pl.ds(r0, ATTN_Q_TILE), :] = (o / l).astype(o_ref.dtype)
        return carry

    lax.fori_loop(0, seq // ATTN_Q_TILE, body, 0)


def _mla_attn(q, k, v, batch, seq):
    n = q.shape[0]
    return pl.pallas_call(
        _mla_attn_kernel,
        grid=(batch, MLA_HEADS),
        in_specs=[pl.BlockSpec((seq, MLA_HEAD_LANES), lambda b, h: (b, h)),
                  pl.BlockSpec((seq, LANES), lambda b, h: (b, h)),
                  pl.BlockSpec((seq, LANES), lambda b, h: (b, MLA_HEADS)),
                  pl.BlockSpec((seq, MLA_V), lambda b, h: (b, h))],
        out_specs=pl.BlockSpec((seq, MLA_V), lambda b, h: (b, h)),
        out_shape=jax.ShapeDtypeStruct((n, MLA_HEADS * MLA_V), BF16),
        scratch_shapes=[pltpu.VMEM((seq, MLA_HEAD_LANES), BF16)],
        compiler_params=_params("parallel", "arbitrary"),
        name="mla_attn",
    )(q, k, k, v)


def _proj_residual_kernel(a_ref, w_ref, h_ref, o_ref):
    o_ref[...] = h_ref[...] + _dot(a_ref[...], w_ref[...])


def _proj_residual(a, w, h):
    n = h.shape[0]
    t = ROW_TILE
    return pl.pallas_call(
        _proj_residual_kernel,
        grid=(n // t,),
        in_specs=[pl.BlockSpec((t, a.shape[1]), lambda i: (i, 0)),
                  pl.BlockSpec(w.shape, lambda i: (0, 0)),
                  pl.BlockSpec((t, D_MODEL), lambda i: (i, 0))],
        out_specs=pl.BlockSpec((t, D_MODEL), lambda i: (i, 0)),
        out_shape=jax.ShapeDtypeStruct((n, D_MODEL), F32),
        compiler_params=_params("parallel"),
        name="proj_residual",
    )(a, w, h)


def _ffn_kernel(h_ref, g_ref, wg_ref, wu_ref, wd_ref, o_ref, xn_ref, acc_ref):
    f = pl.program_id(1)

    @pl.when(f == 0)
    def _():
        xn_ref[...] = _rms(h_ref[...], g_ref[...]).astype(BF16)
        acc_ref[...] = jnp.zeros_like(acc_ref)

    xn = xn_ref[...]
    gate = _dot(xn, wg_ref[...])
    up = _dot(xn, wu_ref[...])
    act = (gate * jax.nn.sigmoid(gate) * up).astype(BF16)
    acc_ref[...] += _dot(act, wd_ref[...])

    @pl.when(f == pl.num_programs(1) - 1)
    def _():
        o_ref[...] = h_ref[...] + acc_ref[...]


def _ffn(h, gain, wg, wu, wd):
    n = h.shape[0]
    t = ROW_TILE
    nf = FFN_DIM // FFN_TILE
    return pl.pallas_call(
        _ffn_kernel,
        grid=(n // t, nf),
        in_specs=[pl.BlockSpec((t, D_MODEL), lambda i, f: (i, 0)),
                  pl.BlockSpec(gain.shape, lambda i, f: (0, 0)),
                  pl.BlockSpec((D_MODEL, FFN_TILE), lambda i, f: (0, f)),
                  pl.BlockSpec((D_MODEL, FFN_TILE), lambda i, f: (0, f)),
                  pl.BlockSpec((FFN_TILE, D_MODEL), lambda i, f: (f, 0))],
        out_specs=pl.BlockSpec((t, D_MODEL), lambda i, f: (i, 0)),
        out_shape=jax.ShapeDtypeStruct((n, D_MODEL), F32),
        scratch_shapes=[pltpu.VMEM((t, D_MODEL), BF16), pltpu.VMEM((t, D_MODEL), F32)],
        compiler_params=_params("parallel", "arbitrary"),
        name="ffn",
    )(h, gain, wg, wu, wd)


def _norm_matmul_kernel(h_ref, g_ref, w_ref, o_ref):
    o_ref[...] = _dot(_rms(h_ref[...], g_ref[...]).astype(BF16), w_ref[...]).astype(o_ref.dtype)


def _norm_matmul(h, gain, w):
    n = h.shape[0]
    t = ROW_TILE
    return pl.pallas_call(
        _norm_matmul_kernel,
        grid=(n // t,),
        in_specs=[pl.BlockSpec((t, D_MODEL), lambda i: (i, 0)),
                  pl.BlockSpec(gain.shape, lambda i: (0, 0)),
                  pl.BlockSpec(w.shape, lambda i: (0, 0))],
        out_specs=pl.BlockSpec((t, w.shape[1]), lambda i: (i, 0)),
        out_shape=jax.ShapeDtypeStruct((n, w.shape[1]), BF16),
        compiler_params=_params("parallel"),
        name="norm_matmul",
    )(h, gain, w)


def _alibi_slope(head):
    return 2.0 ** (-8.0 * (head + 1) / SWA_Q_HEADS)


def _swa_attn_kernel(sink_ref, q_ref, k_ref, v_ref, o_ref):
    i = pl.program_id(1)
    nb = pl.num_programs(1)
    blk = SWA_BLOCK
    pair_shape = (blk, 2 * blk)
    qpos = i * blk + lax.broadcasted_iota(I32, pair_shape, 0)
    col = lax.broadcasted_iota(I32, pair_shape, 1)
    kcol = jnp.where(col < blk, col, col - blk)
    lane = lax.broadcasted_iota(I32, (blk, LANES), 1)
    lo = lane < SWA_HEAD_DIM

    dists, valids, rows = [], [], []
    for j in (-1, 0, 1):
        kb = i + j
        kpos = kb * blk + kcol
        dist = jnp.abs(qpos - kpos)
        dists.append(dist.astype(F32))
        valids.append((dist <= SWA_WINDOW) & (kb >= 0) & (kb < nb))
        rows.append(pl.multiple_of(jnp.clip(kb, 0, nb - 1) * blk, blk))

    for g in range(SWA_KV_HEADS):
        kk, vv = [], []
        for r0 in rows:
            kd = k_ref[pl.ds(r0, blk), g * LANES:(g + 1) * LANES]
            vd = v_ref[pl.ds(r0, blk), g * LANES:(g + 1) * LANES]
            kk.append(jnp.concatenate([jnp.where(lo, kd, 0), jnp.where(lo, 0, kd)], axis=0))
            vv.append(jnp.concatenate([jnp.where(lo, vd, 0), jnp.where(lo, 0, vd)], axis=0))
        for pr in range(SWA_GROUP // 2):
            h0 = g * SWA_GROUP + 2 * pr
            pc = (h0 // 2) * LANES
            qp = q_ref[:, pc:pc + LANES]
            slope = jnp.where(col < blk, _alibi_slope(h0), _alibi_slope(h0 + 1))
            ss = []
            for j in range(3):
                s = _dot_nt(qp, kk[j]) - slope * dists[j]
                ss.append(jnp.where(valids[j], s, NEG_INF))
            ps, ls = [], []
            for half, head in ((0, h0), (1, h0 + 1)):
                sl = slice(half * blk, (half + 1) * blk)
                snk = sink_ref[head]
                m = jnp.maximum(jnp.maximum(jnp.max(ss[0][:, sl], axis=-1, keepdims=True),
                                            jnp.max(ss[1][:, sl], axis=-1, keepdims=True)),
                                jnp.max(ss[2][:, sl], axis=-1, keepdims=True))
                m = jnp.maximum(m, snk)
                p3 = [jnp.exp(ss[j][:, sl] - m) for j in range(3)]
                l = (jnp.sum(p3[0], axis=-1, keepdims=True) + jnp.sum(p3[1], axis=-1, keepdims=True)
                     + jnp.sum(p3[2], axis=-1, keepdims=True) + jnp.exp(snk - m))
                ps.append(p3)
                ls.append(l)
            o = jnp.zeros((blk, LANES), F32)
            for j in range(3):
                pj = jnp.concatenate([ps[0][j], ps[1][j]], axis=1).astype(BF16)
                o = o + _dot(pj, vv[j])
            o = o / jnp.where(lo, ls[0], ls[1])
            o_ref[:, pc:pc + LANES] = o.astype(o_ref.dtype)


def _swa_attn(qkv, sink, batch, seq):
    n = qkv.shape[0]
    nb = seq // SWA_BLOCK
    qw = SWA_Q_HEADS * SWA_HEAD_DIM
    kvw = SWA_KV_HEADS * LANES
    kblk = qw // kvw
    return pl.pallas_call(
        _swa_attn_kernel,
        grid=(batch, nb),
        in_specs=[pl.BlockSpec(memory_space=pltpu.SMEM),
                  pl.BlockSpec((SWA_BLOCK, qw), lambda b, i: (b * nb + i, 0)),
                  pl.BlockSpec((seq, kvw), lambda b, i: (b, kblk)),
                  pl.BlockSpec((seq, kvw), lambda b, i: (b, kblk + 1))],
        out_specs=pl.BlockSpec((SWA_BLOCK, qw), lambda b, i: (b * nb + i, 0)),
        out_shape=jax.ShapeDtypeStruct((n, qw), BF16),
        compiler_params=_params("parallel", "arbitrary"),
        name="swa_attn",
    )(sink, qkv, qkv, qkv)


def _moe_route_kernel(h_ref, g_ref, wr_ref, xs_ref, pos_ref, gate_ref, cnt_ref):
    t = MOE_CHUNK
    xn = _rms(h_ref[...], g_ref[...])
    logits = jnp.dot(xn, wr_ref[...], preferred_element_type=F32, precision=lax.Precision.HIGHEST)
    lt = logits.T[:N_EXPERTS]
    eio = lax.broadcasted_iota(I32, lt.shape, 0)
    m1 = jnp.max(lt, axis=0, keepdims=True)
    i1 = jnp.min(jnp.where(lt == m1, eio, N_EXPERTS), axis=0, keepdims=True)
    l2 = jnp.where(eio == i1, -jnp.inf, lt)
    m2 = jnp.max(l2, axis=0, keepdims=True)
    i2 = jnp.min(jnp.where(l2 == m2, eio, N_EXPERTS), axis=0, keepdims=True)
    e21 = jnp.exp(m2 - m1)
    g1 = 1.0 / (1.0 + e21)
    g2 = e21 * g1

    sel1 = eio == i1
    sel2 = eio == i2
    oh = jnp.where(sel1 | sel2, 1.0, 0.0)
    tr = lax.broadcasted_iota(I32, (t, t), 0)
    tc = lax.broadcasted_iota(I32, (t, t), 1)
    before = jnp.where(tr < tc, 1.0, 0.0).astype(BF16)
    rank = _dot(oh.astype(BF16), before)
    cnt = jnp.sum(oh, axis=1, keepdims=True).astype(I32)
    padded = jnp.bitwise_and(cnt + (MOE_GRANULE - 1), -MOE_GRANULE)
    offs, run = [], jnp.zeros((1, 1), I32)
    for e in range(N_EXPERTS):
        offs.append(run)
        run = run + padded[e:e + 1, :]
    off = jnp.concatenate(offs, axis=0).astype(F32)
    slot = rank + off
    pos1 = jnp.sum(jnp.where(sel1, slot, 0.0), axis=0, keepdims=True).astype(I32)
    pos2 = jnp.sum(jnp.where(sel2, slot, 0.0), axis=0, keepdims=True).astype(I32)

    rio = lax.broadcasted_iota(I32, (MOE_CHUNK_ROWS, t), 0)
    perm = jnp.where((rio == pos1) | (rio == pos2), 1.0, 0.0).astype(BF16)
    xs_ref[...] = _dot(perm, xn.astype(BF16)).astype(BF16)

    zi = jnp.zeros((N_EXPERTS - 2, t), I32)
    pos_ref[...] = jnp.concatenate([pos1, pos2, zi], axis=0)
    gate_ref[...] = jnp.concatenate([g1, g2, zi.astype(F32)], axis=0)
    cnt_ref[...] = jnp.broadcast_to(padded, (N_EXPERTS, LANES))


def _moe_route(h, gain, wr):
    n = h.shape[0]
    t = MOE_CHUNK
    nc = n // t
    return pl.pallas_call(
        _moe_route_kernel,
        grid=(nc,),
        in_specs=[pl.BlockSpec((t, D_MODEL), lambda i: (i, 0)),
                  pl.BlockSpec(gain.shape, lambda i: (0, 0)),
                  pl.BlockSpec(wr.shape, lambda i: (0, 0))],
        out_specs=[pl.BlockSpec((MOE_CHUNK_ROWS, D_MODEL), lambda i: (i, 0)),
                   pl.BlockSpec((None, N_EXPERTS, t), lambda i: (i, 0, 0)),
                   pl.BlockSpec((None, N_EXPERTS, t), lambda i: (i, 0, 0)),
                   pl.BlockSpec((None, N_EXPERTS, LANES), lambda i: (i, 0, 0))],
        out_shape=[jax.ShapeDtypeStruct((nc * MOE_CHUNK_ROWS, D_MODEL), BF16),
                   jax.ShapeDtypeStruct((nc, N_EXPERTS, t), I32),
                   jax.ShapeDtypeStruct((nc, N_EXPERTS, t), F32),
                   jax.ShapeDtypeStruct((nc, N_EXPERTS, LANES), I32)],
        compiler_params=_params("parallel"),
        name="moe_route",
    )(h, gain, wr)


def _moe_plan(padded, n_tokens):
    nc = padded.shape[0]
    g = MOE_GRANULE
    loc_off = jnp.cumsum(padded, axis=1) - padded
    total = jnp.sum(padded, axis=0)
    total_pad = ((total + MOE_ROW_TILE - 1) // MOE_ROW_TILE) * MOE_ROW_TILE
    gbase = jnp.cumsum(total_pad) - total_pad
    coff = jnp.cumsum(padded, axis=0) - padded
    row = jnp.arange(MOE_GRANULES, dtype=I32)[None, :, None] * g
    ends = (loc_off + padded)[:, None, :]
    e = jnp.minimum(jnp.sum((ends <= row).astype(I32), axis=2), N_EXPERTS - 1)
    take = lambda a: jnp.take_along_axis(a, e, axis=1)
    dest = gbase[e] + take(coff) + row[:, :, 0] - take(loc_off)
    n_used = jnp.sum(padded, axis=1) // g
    valid = jnp.arange(MOE_GRANULES, dtype=I32)[None, :] < n_used[:, None]
    gdest = jnp.where(valid, dest // g, 0).astype(I32).reshape(-1)
    n_tiles = _moe_tiles(n_tokens)
    tile_row = jnp.arange(n_tiles, dtype=I32) * MOE_ROW_TILE
    tile_expert = jnp.minimum(jnp.sum(((gbase + total_pad)[None, :] <= tile_row[:, None]).astype(I32), axis=1),
                              N_EXPERTS - 1).astype(I32)
    tiles_used = ((gbase[-1] + total_pad[-1]) // MOE_ROW_TILE).astype(I32).reshape(1)
    return gdest, n_used.astype(I32), tile_expert, tiles_used


def _moe_tiles(n_tokens):
    rows = (n_tokens // MOE_CHUNK) * MOE_CHUNK_ROWS + N_EXPERTS * MOE_ROW_TILE
    return -(-rows // MOE_ROW_TILE)


def _granule_copy(src, dst, sem):
    return pltpu.make_async_copy(src, dst, sem)


def _moe_scatter_kernel(gd_ref, nu_ref, xs_ref, init_ref, out_ref, sem):
    del init_ref
    c = pl.program_id(0)
    n = nu_ref[c]
    g = MOE_GRANULE

    def copy(j):
        d = gd_ref[c * MOE_GRANULES + j]
        return _granule_copy(xs_ref.at[pl.ds(pl.multiple_of(j * g, g), g)],
                             out_ref.at[pl.ds(pl.multiple_of(d * g, g), g)], sem)

    def start(j, carry):
        @pl.when(j < n)
        def _():
            copy(j).start()
        return carry

    def wait(j, carry):
        @pl.when(j < n)
        def _():
            copy(j).wait()
        return carry

    lax.fori_loop(0, MOE_GRANULES, start, 0)
    lax.fori_loop(0, MOE_GRANULES, wait, 0)


def _moe_scatter(gdest, n_used, xs, n_rows):
    nc = n_used.shape[0]
    init = jnp.zeros((n_rows, D_MODEL), BF16)
    return pl.pallas_call(
        _moe_scatter_kernel,
        grid_spec=pltpu.PrefetchScalarGridSpec(
            num_scalar_prefetch=2,
            grid=(nc,),
            in_specs=[pl.BlockSpec((MOE_CHUNK_ROWS, D_MODEL), lambda i, gd, nu: (i, 0)),
                      pl.BlockSpec(memory_space=pl.ANY)],
            out_specs=pl.BlockSpec(memory_space=pl.ANY),
            scratch_shapes=[pltpu.SemaphoreType.DMA(())]),
        out_shape=jax.ShapeDtypeStruct((n_rows, D_MODEL), BF16),
        input_output_aliases={3: 0},
        compiler_params=_params("arbitrary"),
        name="moe_scatter",
    )(gdest, n_used, xs, init)


def _moe_expert_kernel(te_ref, nt_ref, x_ref, wg_ref, wu_ref, wd_ref, o_ref, acc_ref):
    i = pl.program_id(0)
    f = pl.program_id(1)
    last = pl.num_programs(1) - 1
    live = i < nt_ref[0]

    @pl.when(live)
    def _():
        @pl.when(f == 0)
        def _():
            acc_ref[...] = jnp.zeros_like(acc_ref)

        x = x_ref[...]
        gate = _dot(x, wg_ref[...])
        up = _dot(x, wu_ref[...])
        act = (gate * jax.nn.sigmoid(gate) * up).astype(BF16)
        acc_ref[...] += _dot(act, wd_ref[...])

        @pl.when(f == last)
        def _():
            o_ref[...] = acc_ref[...].astype(o_ref.dtype)

    @pl.when(jnp.logical_not(live) & (f == last))
    def _():
        o_ref[...] = jnp.zeros_like(o_ref)


def _moe_expert(tile_expert, tiles_used, xg, wg, wu, wd):
    n_rows = xg.shape[0]
    nt = n_rows // MOE_ROW_TILE
    nf = EXPERT_DIM // MOE_F_TILE

    def tile(i, nt_ref):
        return jnp.minimum(i, nt_ref[0] - 1)

    def fcol(i, f, nt_ref):
        return jnp.where(i < nt_ref[0], f, nf - 1)

    return pl.pallas_call(
        _moe_expert_kernel,
        grid_spec=pltpu.PrefetchScalarGridSpec(
            num_scalar_prefetch=2,
            grid=(nt, nf),
            in_specs=[pl.BlockSpec((MOE_ROW_TILE, D_MODEL), lambda i, f, te, ntr: (tile(i, ntr), 0)),
                      pl.BlockSpec((None, D_MODEL, MOE_F_TILE),
                                   lambda i, f, te, ntr: (te[tile(i, ntr)], 0, fcol(i, f, ntr))),
                      pl.BlockSpec((None, D_MODEL, MOE_F_TILE),
                                   lambda i, f, te, ntr: (te[tile(i, ntr)], 0, fcol(i, f, ntr))),
                      pl.BlockSpec((None, MOE_F_TILE, D_MODEL),
                                   lambda i, f, te, ntr: (te[tile(i, ntr)], fcol(i, f, ntr), 0))],
            out_specs=pl.BlockSpec((MOE_ROW_TILE, D_MODEL), lambda i, f, te, ntr: (i, 0)),
            scratch_shapes=[pltpu.VMEM((MOE_ROW_TILE, D_MODEL), F32)]),
        out_shape=jax.ShapeDtypeStruct((n_rows, D_MODEL), BF16),
        compiler_params=_params("arbitrary", "arbitrary"),
        name="moe_expert",
    )(tile_expert, tiles_used, xg, wg, wu, wd)


def _moe_combine_kernel(gd_ref, nu_ref, h_ref, pos_ref, gate_ref, fg_ref, ys_ref, o_ref, ybuf_ref, sem,
                        *, final_norm):
    c = pl.program_id(0)
    n = nu_ref[c]
    g = MOE_GRANULE
    t = MOE_CHUNK

    def copy(j):
        d = gd_ref[c * MOE_GRANULES + j]
        return _granule_copy(ys_ref.at[pl.ds(pl.multiple_of(d * g, g), g)],
                             ybuf_ref.at[pl.ds(pl.multiple_of(j * g, g), g)], sem)

    def start(j, carry):
        @pl.when(j < n)
        def _():
            copy(j).start()

        @pl.when(j >= n)
        def _():
            ybuf_ref[pl.ds(pl.multiple_of(j * g, g), g), :] = jnp.zeros((g, D_MODEL), BF16)
        return carry

    def wait(j, carry):
        @pl.when(j < n)
        def _():
            copy(j).wait()
        return carry

    lax.fori_loop(0, MOE_GRANULES, start, 0)
    lax.fori_loop(0, MOE_GRANULES, wait, 0)

    rio = lax.broadcasted_iota(I32, (MOE_CHUNK_ROWS, t), 0)
    w = (jnp.where(rio == pos_ref[0:1, :], gate_ref[0:1, :], 0.0)
         + jnp.where(rio == pos_ref[1:2, :], gate_ref[1:2, :], 0.0)).astype(BF16)
    y = lax.dot_general(w, ybuf_ref[...], (((0,), (0,)), ((), ())), preferred_element_type=F32)
    out = h_ref[...] + y
    if final_norm:
        out = _rms(out, fg_ref[...])
    o_ref[...] = out


def _moe_combine(gdest, n_used, h, pos, gates, ys, final_gain, final_norm):
    n = h.shape[0]
    t = MOE_CHUNK
    nc = n // t
    return pl.pallas_call(
        functools.partial(_moe_combine_kernel, final_norm=final_norm),
        grid_spec=pltpu.PrefetchScalarGridSpec(
            num_scalar_prefetch=2,
            grid=(nc,),
            in_specs=[pl.BlockSpec((t, D_MODEL), lambda i, gd, nu: (i, 0)),
                      pl.BlockSpec((None, N_EXPERTS, t), lambda i, gd, nu: (i, 0, 0)),
                      pl.BlockSpec((None, N_EXPERTS, t), lambda i, gd, nu: (i, 0, 0)),
                      pl.BlockSpec(final_gain.shape, lambda i, gd, nu: (0, 0)),
                      pl.BlockSpec(memory_space=pl.ANY)],
            out_specs=pl.BlockSpec((t, D_MODEL), lambda i, gd, nu: (i, 0)),
            scratch_shapes=[pltpu.VMEM((MOE_CHUNK_ROWS, D_MODEL), BF16), pltpu.SemaphoreType.DMA(())]),
        out_shape=jax.ShapeDtypeStruct((n, D_MODEL), F32),
        compiler_params=_params("arbitrary"),
        name="moe_combine",
    )(gdest, n_used, h, pos, gates, final_gain, ys)


def _moe(h, gain, router, wg, wu, wd, final_gain, final_norm):
    n = h.shape[0]
    wr = jnp.pad(router, ((0, 0), (0, LANES - N_EXPERTS)))
    xs, pos, gates, cnt = _moe_route(h, gain, wr)
    gdest, n_used, tile_expert, tiles_used = _moe_plan(cnt[:, :, 0], n)
    xg = _moe_scatter(gdest, n_used, xs, _moe_tiles(n) * MOE_ROW_TILE)
    ys = _moe_expert(tile_expert, tiles_used, xg, wg.astype(BF16), wu.astype(BF16), wd.astype(BF16))
    return _moe_combine(gdest, n_used, h, pos, gates, ys, final_gain, final_norm)


def _rot_cols(w):
    half = MLA_ROPE // 2
    return jnp.concatenate([-w[..., half:], w[..., :half]], axis=-1)


def _rope_table(seq):
    inv = 1.0 / (ROPE_THETA ** (jnp.arange(0, MLA_ROPE, 2, dtype=F32) / MLA_ROPE))
    ang = jnp.arange(seq, dtype=F32)[:, None] * inv[None, :]
    cos, sin = jnp.cos(ang), jnp.sin(ang)
    return jnp.concatenate([cos, cos, sin, sin], axis=-1)


def _mla_weights(w_dqkv, w_uq, w_uk, w_uv):
    lat = MLA_Q_RANK + MLA_KV_RANK
    rope = w_dqkv[:, lat:]
    wd = jnp.concatenate([w_dqkv[:, :lat], rope, _rot_cols(rope)], axis=1).astype(BF16)
    q_rope = w_uq[..., MLA_NOPE:]
    wuq = jnp.concatenate([w_uq[..., :MLA_NOPE], q_rope, _rot_cols(q_rope)], axis=-1)
    wuq = wuq.reshape(MLA_Q_RANK, MLA_HEADS * MLA_HEAD_LANES).astype(BF16)
    wukv = jnp.concatenate([w_uk.reshape(MLA_KV_RANK, -1), w_uv.reshape(MLA_KV_RANK, -1)], axis=1).astype(BF16)
    return wd, wuq, wukv


def _swa_weights(w_qkv):
    qw = SWA_Q_HEADS * SWA_HEAD_DIM
    kvw = SWA_KV_HEADS * SWA_HEAD_DIM
    dup = lambda w: jnp.concatenate([w.reshape(D_MODEL, SWA_KV_HEADS, 1, SWA_HEAD_DIM)] * 2, axis=2).reshape(D_MODEL, -1)
    wq = w_qkv[:, :qw] * (SWA_HEAD_DIM ** -0.5)
    return jnp.concatenate([wq, dup(w_qkv[:, qw:qw + kvw]), dup(w_qkv[:, qw + kvw:])], axis=1).astype(BF16)


def kernel(x, mla_norm, mla_w_dqkv, mla_q_norm, mla_w_uq, mla_kv_norm, mla_w_uk, mla_w_uv, mla_w_o, swa_norm, swa_w_qkv, swa_sink, swa_w_o, ffn_norm, ffn_w_gate, ffn_w_up, ffn_w_down, moe_norm, moe_router, moe_w_gate, moe_w_up, moe_w_down, final_norm):
    batch, seq, _ = x.shape
    h = x.reshape(batch * seq, D_MODEL)
    cs = _rope_table(seq)
    row = lambda v: v.reshape(1, -1).astype(F32)
    depth = 2 * mla_norm.shape[0]
    for layer in range(depth):
        j = layer // 2
        if layer % 2 == 0:
            wd, wuq, wukv = _mla_weights(mla_w_dqkv[j], mla_w_uq[j], mla_w_uk[j], mla_w_uv[j])
            q, k, v = _mla_proj(h, seq, row(mla_norm[j]), wd, row(mla_q_norm[j]), row(mla_kv_norm[j]), wuq, wukv, cs)
            a = _mla_attn(q, k, v, batch, seq)
            h = _proj_residual(a, mla_w_o[j].astype(BF16), h)
            h = _ffn(h, row(ffn_norm[j]), ffn_w_gate[j].astype(BF16), ffn_w_up[j].astype(BF16),
                     ffn_w_down[j].astype(BF16))
        else:
            qkv = _norm_matmul(h, row(swa_norm[j]), _swa_weights(swa_w_qkv[j]))
            a = _swa_attn(qkv, swa_sink[j].astype(F32), batch, seq)
            h = _proj_residual(a, swa_w_o[j].astype(BF16), h)
            h = _moe(h, row(moe_norm[j]), moe_router[j], moe_w_gate[j], moe_w_up[j], moe_w_down[j],
                     row(final_norm), layer == depth - 1)
    return h.reshape(batch, seq, D_MODEL)
```

```python
import functools

import jax
import jax.numpy as jnp
from jax import lax
from jax.experimental import pallas as pl
from jax.experimental.pallas import tpu as pltpu

F32 = jnp.float32
BF16 = jnp.bfloat16
I32 = jnp.int32

D_MODEL = 1024
EPS = 1e-6
NEG_INF = -1e30
ROPE_THETA = 10000.0
LOG2_E = 1.4426950408889634

MLA_HEADS = 8
MLA_Q_RANK = 256
MLA_KV_RANK = 256
MLA_NOPE = 128
MLA_ROPE = 64
MLA_V = 128
MLA_QK = MLA_NOPE + MLA_ROPE
MLA_HEAD_LANES = 256

SWA_Q_HEADS = 16
SWA_KV_HEADS = 4
SWA_GROUP = 4
SWA_HEAD_DIM = 64
SWA_WINDOW = 128
SWA_BLOCK = 128
SWA_LOOKAHEAD = 2

FFN_DIM = 2816
N_EXPERTS = 8
EXPERT_DIM = 2048

LANES = 128
VMEM_LIMIT = 48 * 1024 * 1024

ROW_TILE = 512
ATTN_Q_TILE = 512
FFN_TILE = 1408
MOE_CHUNK = 512
MOE_GRANULE = 16
MOE_CHUNK_ROWS = 2 * MOE_CHUNK + N_EXPERTS * MOE_GRANULE
MOE_GRANULES = MOE_CHUNK_ROWS // MOE_GRANULE
MOE_DMA_UNROLL = 8
MOE_ROW_TILE = 512
MOE_F_TILE = 1024


def _params(*sem, flags=None):
    return pltpu.CompilerParams(dimension_semantics=sem, vmem_limit_bytes=VMEM_LIMIT, flags=flags)


def _rms(x, g):
    return x * lax.rsqrt(jnp.mean(x * x, axis=-1, keepdims=True) + EPS) * g


def _dot(a, b):
    return jnp.dot(a, b, preferred_element_type=F32)


def _dot_nt(a, b):
    return lax.dot_general(a, b, (((1,), (1,)), ((), ())), preferred_element_type=F32)


def _mla_proj_kernel(h_ref, g_ref, wd_ref, qn_ref, kvn_ref, wuq_ref, wuk_ref, wuvt_ref, cs_ref,
                     q_ref, k_ref, vt_ref):
    xn = _rms(h_ref[...], g_ref[...]).astype(BF16)
    down = _dot(xn, wd_ref[...])
    cq = _rms(down[:, :MLA_Q_RANK], qn_ref[...]).astype(BF16)
    ckv = _rms(down[:, MLA_Q_RANK:MLA_Q_RANK + MLA_KV_RANK], kvn_ref[...]).astype(BF16)
    cs = cs_ref[...]
    lane = lax.broadcasted_iota(I32, cs.shape, 1)

    def rope(a):
        p = a * cs
        return p + pltpu.roll(p, MLA_ROPE, 1)

    kr = jnp.where(lane < MLA_ROPE, rope(down[:, 2 * MLA_Q_RANK:]), 0.0)
    q = _dot(cq, wuq_ref[...]) * (MLA_QK ** -0.5 * LOG2_E)
    for h in range(MLA_HEADS):
        c = h * MLA_HEAD_LANES
        q_ref[:, c:c + LANES] = q[:, c:c + LANES].astype(BF16)
        q_ref[:, c + LANES:c + 2 * LANES] = rope(q[:, c + LANES:c + 2 * LANES]).astype(BF16)
    nk = MLA_HEADS * MLA_NOPE
    k_ref[:, :nk] = _dot(ckv, wuk_ref[...]).astype(BF16)
    k_ref[:, nk:] = kr.astype(BF16)
    vt_ref[...] = _dot_nt(wuvt_ref[...], ckv).astype(BF16)


def _mla_proj(h, batch, seq, gain, wd, qn, kvn, wuq, wuk, wuvt, cs):
    n = h.shape[0]
    t = ROW_TILE
    sb = seq // t
    full = lambda a: pl.BlockSpec(a.shape, lambda i: (0, 0))
    return pl.pallas_call(
        _mla_proj_kernel,
        grid=(n // t,),
        in_specs=[pl.BlockSpec((t, D_MODEL), lambda i: (i, 0)), full(gain), full(wd), full(qn), full(kvn),
                  full(wuq), full(wuk), full(wuvt), pl.BlockSpec((t, LANES), lambda i: (i % sb, 0))],
        out_specs=[pl.BlockSpec((t, MLA_HEADS * MLA_HEAD_LANES), lambda i: (i, 0)),
                   pl.BlockSpec((t, MLA_HEADS * MLA_NOPE + LANES), lambda i: (i, 0)),
                   pl.BlockSpec((None, MLA_HEADS * MLA_V, t), lambda i: (i // sb, 0, i % sb))],
        out_shape=[jax.ShapeDtypeStruct((n, MLA_HEADS * MLA_HEAD_LANES), BF16),
                   jax.ShapeDtypeStruct((n, MLA_HEADS * MLA_NOPE + LANES), BF16),
                   jax.ShapeDtypeStruct((batch, MLA_HEADS * MLA_V, seq), BF16)],
        compiler_params=_params("parallel"),
        name="mla_proj",
    )(h, gain, wd, qn, kvn, wuq, wuk, wuvt, cs)


def _mla_attn_kernel(q_ref, kn_ref, kr_ref, vt_ref, o_ref, kcat_ref):
    kcat_ref[:, :LANES] = kn_ref[...]
    kcat_ref[:, LANES:] = kr_ref[...]
    seq = q_ref.shape[0]

    tq = ATTN_Q_TILE
    n_tiles = seq // tq

    def scores(t):
        return _dot_nt(kcat_ref[...], q_ref[t * tq:(t + 1) * tq, :])

    def attend(st):
        pt = jnp.exp2(st - jnp.max(st, axis=0, keepdims=True))
        l = jnp.sum(pt, axis=0, keepdims=True)
        ot = _dot(vt_ref[...], pt.astype(BF16)) / l
        return ot.T.astype(o_ref.dtype)

    pending = scores(0)
    outs = []
    for t in range(n_tiles):
        st = pending
        if t + 1 < n_tiles:
            pending = scores(t + 1)
        outs.append(attend(st))
    o_ref[...] = jnp.concatenate(outs, axis=0)


def _mla_attn(q, k, vt, batch, seq):
    n = q.shape[0]
    return pl.pallas_call(
        _mla_attn_kernel,
        grid=(batch, MLA_HEADS),
        in_specs=[pl.BlockSpec((seq, MLA_HEAD_LANES), lambda b, h: (b, h)),
                  pl.BlockSpec((seq, LANES), lambda b, h: (b, h)),
                  pl.BlockSpec((seq, LANES), lambda b, h: (b, MLA_HEADS)),
                  pl.BlockSpec((None, MLA_V, seq), lambda b, h: (b, h, 0))],
        out_specs=pl.BlockSpec((seq, MLA_V), lambda b, h: (b, h)),
        out_shape=jax.ShapeDtypeStruct((n, MLA_HEADS * MLA_V), BF16),
        scratch_shapes=[pltpu.VMEM((seq, MLA_HEAD_LANES), BF16)],
        compiler_params=_params("parallel", "arbitrary"),
        name="mla_attn",
    )(q, k, k, vt)


def _ffn_kernel(a_ref, wo_ref, h_ref, g_ref, wg_ref, wu_ref, wd_ref, o_ref, xn_ref, acc_ref):
    f = pl.program_id(1)

    @pl.when(f == 0)
    def _():
        h1 = h_ref[...] + _dot(a_ref[...], wo_ref[...])
        xn_ref[...] = _rms(h1, g_ref[...]).astype(BF16)
        acc_ref[...] = h1

    xn = xn_ref[...]
    gate = _dot(xn, wg_ref[...])
    up = _dot(xn, wu_ref[...])
    act = (gate * jax.nn.sigmoid(gate) * up).astype(BF16)
    acc_ref[...] += _dot(act, wd_ref[...])

    @pl.when(f == pl.num_programs(1) - 1)
    def _():
        o_ref[...] = acc_ref[...]


def _ffn(a, wo, h, gain, wg, wu, wd):
    n = h.shape[0]
    t = ROW_TILE
    nf = FFN_DIM // FFN_TILE
    return pl.pallas_call(
        _ffn_kernel,
        grid=(n // t, nf),
        in_specs=[pl.BlockSpec((t, a.shape[1]), lambda i, f: (i, 0)),
                  pl.BlockSpec(wo.shape, lambda i, f: (0, 0)),
                  pl.BlockSpec((t, D_MODEL), lambda i, f: (i, 0)),
                  pl.BlockSpec(gain.shape, lambda i, f: (0, 0)),
                  pl.BlockSpec((D_MODEL, FFN_TILE), lambda i, f: (0, f)),
                  pl.BlockSpec((D_MODEL, FFN_TILE), lambda i, f: (0, f)),
                  pl.BlockSpec((FFN_TILE, D_MODEL), lambda i, f: (f, 0))],
        out_specs=pl.BlockSpec((t, D_MODEL), lambda i, f: (i, 0)),
        out_shape=jax.ShapeDtypeStruct((n, D_MODEL), F32),
        scratch_shapes=[pltpu.VMEM((t, D_MODEL), BF16), pltpu.VMEM((t, D_MODEL), F32)],
        compiler_params=_params("parallel", "arbitrary"),
        name="ffn",
    )(a, wo, h, gain, wg, wu, wd)


def _norm_matmul_kernel(h_ref, g_ref, w_ref, o_ref):
    o_ref[...] = _dot(_rms(h_ref[...], g_ref[...]).astype(BF16), w_ref[...]).astype(o_ref.dtype)


def _norm_matmul(h, gain, w):
    n = h.shape[0]
    t = ROW_TILE
    return pl.pallas_call(
        _norm_matmul_kernel,
        grid=(n // t,),
        in_specs=[pl.BlockSpec((t, D_MODEL), lambda i: (i, 0)),
                  pl.BlockSpec(gain.shape, lambda i: (0, 0)),
                  pl.BlockSpec(w.shape, lambda i: (0, 0))],
        out_specs=pl.BlockSpec((t, w.shape[1]), lambda i: (i, 0)),
        out_shape=jax.ShapeDtypeStruct((n, w.shape[1]), BF16),
        compiler_params=_params("parallel"),
        name="norm_matmul",
    )(h, gain, w)


def _alibi_slope(head):
    return 2.0 ** (-8.0 * (head + 1) / SWA_Q_HEADS)


def _swa_attn_kernel(sink_ref, q_ref, k_ref, v_ref, o_ref):
    i = pl.program_id(1)
    nb = pl.num_programs(1)
    blk = SWA_BLOCK
    pair_shape = (blk, 2 * blk)
    qpos = i * blk + lax.broadcasted_iota(I32, pair_shape, 0)
    col = lax.broadcasted_iota(I32, pair_shape, 1)
    kcol = jnp.where(col < blk, col, col - blk)
    lane = lax.broadcasted_iota(I32, (blk, LANES), 1)
    lo = lane < SWA_HEAD_DIM

    neg_dist, rows = [], []
    for j in (-1, 0, 1):
        kb = i + j
        kpos = kb * blk + kcol
        dist = jnp.abs(qpos - kpos)
        valid = (dist <= SWA_WINDOW) & (kb >= 0) & (kb < nb)
        neg_dist.append(jnp.where(valid, -dist.astype(F32), NEG_INF))
        rows.append(pl.multiple_of(jnp.clip(kb, 0, nb - 1) * blk, blk))

    def split(ref, g, r0):
        d = ref[pl.ds(r0, blk), g * LANES:(g + 1) * LANES]
        return jnp.concatenate([jnp.where(lo, d, 0), jnp.where(lo, 0, d)], axis=0)

    def scores(pair):
        h0 = 2 * pair
        qp = q_ref[:, pair * LANES:(pair + 1) * LANES]
        slope = jnp.where(col[0:1, :] < blk, _alibi_slope(h0) * LOG2_E, _alibi_slope(h0 + 1) * LOG2_E)
        return [_dot_nt(qp, split(k_ref, h0 // SWA_GROUP, rows[j])) + slope * neg_dist[j] for j in range(3)]

    def attend(pair, ss):
        h0 = 2 * pair
        ps, ls = [], []
        for half, head in ((0, h0), (1, h0 + 1)):
            sl = slice(half * blk, (half + 1) * blk)
            snk = sink_ref[head] * LOG2_E
            m = jnp.max(jnp.maximum(jnp.maximum(ss[0][:, sl], ss[1][:, sl]), ss[2][:, sl]), axis=-1, keepdims=True)
            m = jnp.maximum(m, snk)
            p3 = [jnp.exp2(ss[j][:, sl] - m) for j in range(3)]
            ls.append(jnp.sum(p3[0] + p3[1] + p3[2], axis=-1, keepdims=True) + jnp.exp2(snk - m))
            ps.append(p3)
        o = jnp.zeros((blk, LANES), F32)
        for j in range(3):
            pj = jnp.concatenate([ps[0][j], ps[1][j]], axis=1).astype(BF16)
            o = o + _dot(pj, split(v_ref, h0 // SWA_GROUP, rows[j]))
        return (o / jnp.where(lo, ls[0], ls[1])).astype(o_ref.dtype)

    n_pairs = SWA_Q_HEADS // 2
    pending = {p: scores(p) for p in range(min(SWA_LOOKAHEAD, n_pairs))}
    outs = []
    for p in range(n_pairs):
        if p + SWA_LOOKAHEAD < n_pairs:
            pending[p + SWA_LOOKAHEAD] = scores(p + SWA_LOOKAHEAD)
        outs.append(attend(p, pending.pop(p)))
    o_ref[...] = jnp.concatenate(outs, axis=1)


def _swa_attn(qkv, sink, batch, seq):
    n = qkv.shape[0]
    nb = seq // SWA_BLOCK
    qw = SWA_Q_HEADS * SWA_HEAD_DIM
    kvw = SWA_KV_HEADS * LANES
    kblk = qw // kvw
    return pl.pallas_call(
        _swa_attn_kernel,
        grid=(batch, nb),
        in_specs=[pl.BlockSpec(memory_space=pltpu.SMEM),
                  pl.BlockSpec((SWA_BLOCK, qw), lambda b, i: (b * nb + i, 0)),
                  pl.BlockSpec((seq, kvw), lambda b, i: (b, kblk)),
                  pl.BlockSpec((seq, kvw), lambda b, i: (b, kblk + 1))],
        out_specs=pl.BlockSpec((SWA_BLOCK, qw), lambda b, i: (b * nb + i, 0)),
        out_shape=jax.ShapeDtypeStruct((n, qw), BF16),
        compiler_params=_params("parallel", "arbitrary"),
        name="swa_attn",
    )(sink, qkv, qkv, qkv)


def _moe_route_kernel(a_ref, wo_ref, h_ref, g_ref, wr_ref, h1_ref, xs_ref, pos_ref, gate_ref, cnt_ref):
    t = MOE_CHUNK
    h1 = h_ref[...] + _dot(a_ref[...], wo_ref[...])
    h1_ref[...] = h1
    xn = _rms(h1, g_ref[...])
    x_hi = xn.astype(BF16)
    x_lo = (xn - x_hi.astype(F32)).astype(BF16)
    hi = _dot(x_hi, wr_ref[...])
    logits = hi[:, :LANES] + hi[:, LANES:] + _dot(x_lo, wr_ref[:, :LANES])
    lt = logits.T[:N_EXPERTS]
    eio = lax.broadcasted_iota(I32, lt.shape, 0)
    m1 = jnp.max(lt, axis=0, keepdims=True)
    i1 = jnp.min(jnp.where(lt == m1, eio, N_EXPERTS), axis=0, keepdims=True)
    l2 = jnp.where(eio == i1, -jnp.inf, lt)
    m2 = jnp.max(l2, axis=0, keepdims=True)
    i2 = jnp.min(jnp.where(l2 == m2, eio, N_EXPERTS), axis=0, keepdims=True)
    e21 = jnp.exp(m2 - m1)
    g1 = 1.0 / (1.0 + e21)
    g2 = e21 * g1

    sel1 = eio == i1
    sel2 = eio == i2
    oh = jnp.where(sel1 | sel2, 1.0, 0.0)
    tr = lax.broadcasted_iota(I32, (t, t), 0)
    tc = lax.broadcasted_iota(I32, (t, t), 1)
    before = jnp.where(tr < tc, 1.0, 0.0).astype(BF16)
    rank = _dot(oh.astype(BF16), before)
    cnt = jnp.sum(oh, axis=1, keepdims=True).astype(I32)
    padded = jnp.bitwise_and(cnt + (MOE_GRANULE - 1), -MOE_GRANULE)
    offs, run = [], jnp.zeros((1, 1), I32)
    for e in range(N_EXPERTS):
        offs.append(run)
        run = run + padded[e:e + 1, :]
    off = jnp.concatenate(offs, axis=0).astype(F32)
    slot = rank + off
    pos1 = jnp.sum(jnp.where(sel1, slot, 0.0), axis=0, keepdims=True).astype(I32)
    pos2 = jnp.sum(jnp.where(sel2, slot, 0.0), axis=0, keepdims=True).astype(I32)

    rio = lax.broadcasted_iota(I32, (MOE_CHUNK_ROWS, t), 0)
    perm = jnp.where(rio == pos1, 1.0, jnp.where(rio == pos2, 1.0, 0.0)).astype(BF16)
    xs_ref[...] = _dot(perm, x_hi).astype(BF16)

    zi = jnp.zeros((N_EXPERTS - 2, t), I32)
    pos_ref[...] = jnp.concatenate([pos1, pos2, zi], axis=0)
    gate_ref[...] = jnp.concatenate([g1, g2, zi.astype(F32)], axis=0)
    cnt_ref[...] = jnp.broadcast_to(padded, (N_EXPERTS, LANES))


def _moe_route(a, wo, h, gain, wr):
    n = h.shape[0]
    t = MOE_CHUNK
    nc = n // t
    return pl.pallas_call(
        _moe_route_kernel,
        grid=(nc,),
        in_specs=[pl.BlockSpec((t, a.shape[1]), lambda i: (i, 0)),
                  pl.BlockSpec(wo.shape, lambda i: (0, 0)),
                  pl.BlockSpec((t, D_MODEL), lambda i: (i, 0)),
                  pl.BlockSpec(gain.shape, lambda i: (0, 0)),
                  pl.BlockSpec(wr.shape, lambda i: (0, 0))],
        out_specs=[pl.BlockSpec((t, D_MODEL), lambda i: (i, 0)),
                   pl.BlockSpec((MOE_CHUNK_ROWS, D_MODEL), lambda i: (i, 0)),
                   pl.BlockSpec((None, N_EXPERTS, t), lambda i: (i, 0, 0)),
                   pl.BlockSpec((None, N_EXPERTS, t), lambda i: (i, 0, 0)),
                   pl.BlockSpec((None, N_EXPERTS, LANES), lambda i: (i, 0, 0))],
        out_shape=[jax.ShapeDtypeStruct((n, D_MODEL), F32),
                   jax.ShapeDtypeStruct((nc * MOE_CHUNK_ROWS, D_MODEL), BF16),
                   jax.ShapeDtypeStruct((nc, N_EXPERTS, t), I32),
                   jax.ShapeDtypeStruct((nc, N_EXPERTS, t), F32),
                   jax.ShapeDtypeStruct((nc, N_EXPERTS, LANES), I32)],
        compiler_params=_params("parallel"),
        name="moe_route",
    )(a, wo, h, gain, wr)


def _moe_plan(padded, n_tokens):
    nc = padded.shape[0]
    g = MOE_GRANULE
    loc_off = jnp.cumsum(padded, axis=1) - padded
    total = jnp.sum(padded, axis=0)
    total_pad = ((total + MOE_ROW_TILE - 1) // MOE_ROW_TILE) * MOE_ROW_TILE
    gbase = jnp.cumsum(total_pad) - total_pad
    coff = jnp.cumsum(padded, axis=0) - padded
    shift = gbase[None, :] + coff - loc_off
    step = shift[:, 1:] - shift[:, :-1]
    ends = (loc_off + padded)[:, :-1]
    row = jnp.arange(MOE_GRANULES, dtype=I32)[None, :, None] * g
    dest = row[:, :, 0] + shift[:, :1] + jnp.sum(jnp.where(ends[:, None, :] <= row, step[:, None, :], 0), axis=2)
    n_used = jnp.sum(padded, axis=1) // g
    valid = jnp.arange(MOE_GRANULES, dtype=I32)[None, :] < n_used[:, None]
    gdest = jnp.where(valid, dest // g, 0).astype(I32).reshape(-1)
    n_tiles = _moe_tiles(n_tokens)
    tile_row = jnp.arange(n_tiles, dtype=I32) * MOE_ROW_TILE
    tile_expert = jnp.minimum(jnp.sum(((gbase + total_pad)[None, :] <= tile_row[:, None]).astype(I32), axis=1),
                              N_EXPERTS - 1).astype(I32)
    tiles_used = ((gbase[-1] + total_pad[-1]) // MOE_ROW_TILE).astype(I32).reshape(1)
    tail_start = ((gbase + total) // g).astype(I32)
    tail_count = ((total_pad - total) // g).astype(I32)
    return gdest, n_used.astype(I32), tile_expert, tiles_used, tail_start, tail_count


def _moe_tiles(n_tokens):
    rows = (n_tokens // MOE_CHUNK) * MOE_CHUNK_ROWS + N_EXPERTS * MOE_ROW_TILE
    return -(-rows // MOE_ROW_TILE)


def _granule_copy(src, dst, sem):
    return pltpu.make_async_copy(src, dst, sem)


def _moe_scatter_kernel(gd_ref, nu_ref, ts_ref, tc_ref, nt_ref, xs_ref, out_ref, zero_ref, sem, zsem, tsem):
    g = MOE_GRANULE
    nc = nu_ref.shape[0]
    n_tiles = out_ref.shape[0] // MOE_ROW_TILE
    zero_ref[...] = jnp.zeros_like(zero_ref)

    def zero_copy(e, k):
        return _granule_copy(zero_ref.at[pl.ds(0, g)],
                             out_ref.at[pl.ds(pl.multiple_of((ts_ref[e] + k) * g, g), g)], zsem)

    def for_tail(action):
        for e in range(N_EXPERTS):
            def step(k, carry):
                @pl.when(k < tc_ref[e])
                def _():
                    action(zero_copy(e, k))
                return carry
            lax.fori_loop(0, MOE_ROW_TILE // g, step, 0, unroll=MOE_DMA_UNROLL)

    def for_unused_tiles(action):
        def step(i, carry):
            @pl.when(i >= nt_ref[0])
            def _():
                action(_granule_copy(
                    zero_ref, out_ref.at[pl.ds(pl.multiple_of(i * MOE_ROW_TILE, MOE_ROW_TILE), MOE_ROW_TILE)], tsem))
            return carry
        lax.fori_loop(0, n_tiles, step, 0)

    for_tail(lambda cp: cp.start())
    for_unused_tiles(lambda cp: cp.start())

    def copy(c, j):
        d = gd_ref[c * MOE_GRANULES + j]
        return _granule_copy(xs_ref.at[pl.ds(pl.multiple_of((c * MOE_GRANULES + j) * g, g), g)],
                             out_ref.at[pl.ds(pl.multiple_of(d * g, g), g)], sem)

    def for_used(c, action):
        def step(j, carry):
            @pl.when(j < nu_ref[c])
            def _():
                action(copy(c, j))
            return carry
        lax.fori_loop(0, MOE_GRANULES, step, 0, unroll=MOE_DMA_UNROLL)

    def chunk(c, carry):
        for_used(c, lambda cp: cp.start())

        @pl.when(c > 0)
        def _():
            for_used(c - 1, lambda cp: cp.wait())
        return carry

    lax.fori_loop(0, nc, chunk, 0)
    for_used(nc - 1, lambda cp: cp.wait())
    for_tail(lambda cp: cp.wait())
    for_unused_tiles(lambda cp: cp.wait())


def _moe_scatter(gdest, n_used, tail_start, tail_count, tiles_used, xs, n_rows):
    return pl.pallas_call(
        _moe_scatter_kernel,
        grid_spec=pltpu.PrefetchScalarGridSpec(
            num_scalar_prefetch=5,
            grid=(1,),
            in_specs=[pl.BlockSpec(memory_space=pl.ANY)],
            out_specs=pl.BlockSpec(memory_space=pl.ANY),
            scratch_shapes=[pltpu.VMEM((MOE_ROW_TILE, D_MODEL), BF16), pltpu.SemaphoreType.DMA(()),
                            pltpu.SemaphoreType.DMA(()), pltpu.SemaphoreType.DMA(())]),
        out_shape=jax.ShapeDtypeStruct((n_rows, D_MODEL), BF16),
        compiler_params=_params("arbitrary"),
        name="moe_scatter",
    )(gdest, n_used, tail_start, tail_count, tiles_used, xs)


def _moe_expert_kernel(te_ref, nt_ref, x_ref, wg_ref, wu_ref, wd_ref, o_ref, acc_ref):
    i = pl.program_id(0)
    f = pl.program_id(1)
    last = pl.num_programs(1) - 1
    live = i < nt_ref[0]

    @pl.when(live)
    def _():
        @pl.when(f == 0)
        def _():
            acc_ref[...] = jnp.zeros_like(acc_ref)

        x = x_ref[...]
        gate = _dot(x, wg_ref[...])
        up = _dot(x, wu_ref[...])
        act = (gate * jax.nn.sigmoid(gate) * up).astype(BF16)
        acc_ref[...] += _dot(act, wd_ref[...])

        @pl.when(f == last)
        def _():
            o_ref[...] = acc_ref[...].astype(o_ref.dtype)

    @pl.when(jnp.logical_not(live) & (f == last))
    def _():
        o_ref[...] = jnp.zeros_like(o_ref)


def _moe_expert(tile_expert, tiles_used, xg, wg, wu, wd):
    n_rows = xg.shape[0]
    nt = n_rows // MOE_ROW_TILE
    nf = EXPERT_DIM // MOE_F_TILE

    def tile(i, nt_ref):
        return jnp.minimum(i, nt_ref[0] - 1)

    def fcol(i, f, nt_ref):
        return jnp.where(i < nt_ref[0], f, nf - 1)

    return pl.pallas_call(
        _moe_expert_kernel,
        grid_spec=pltpu.PrefetchScalarGridSpec(
            num_scalar_prefetch=2,
            grid=(nt, nf),
            in_specs=[pl.BlockSpec((MOE_ROW_TILE, D_MODEL), lambda i, f, te, ntr: (tile(i, ntr), 0)),
                      pl.BlockSpec((None, D_MODEL, MOE_F_TILE),
                                   lambda i, f, te, ntr: (te[tile(i, ntr)], 0, fcol(i, f, ntr))),
                      pl.BlockSpec((None, D_MODEL, MOE_F_TILE),
                                   lambda i, f, te, ntr: (te[tile(i, ntr)], 0, fcol(i, f, ntr))),
                      pl.BlockSpec((None, MOE_F_TILE, D_MODEL),
                                   lambda i, f, te, ntr: (te[tile(i, ntr)], fcol(i, f, ntr), 0))],
            out_specs=pl.BlockSpec((MOE_ROW_TILE, D_MODEL), lambda i, f, te, ntr: (i, 0)),
            scratch_shapes=[pltpu.VMEM((MOE_ROW_TILE, D_MODEL), F32)]),
        out_shape=jax.ShapeDtypeStruct((n_rows, D_MODEL), BF16),
        compiler_params=_params("arbitrary", "arbitrary"),
        name="moe_expert",
    )(tile_expert, tiles_used, xg, wg, wu, wd)


def _moe_combine_kernel(gd_ref, nu_ref, h_ref, pos_ref, gate_ref, fg_ref, ys_ref, o_ref, ybuf_ref, sem,
                        *, final_norm):
    c = pl.program_id(0)
    nc = pl.num_programs(0)
    g = MOE_GRANULE
    t = MOE_CHUNK
    slot = c % 2

    def copy(chunk, j):
        d = gd_ref[chunk * MOE_GRANULES + j]
        return _granule_copy(ys_ref.at[pl.ds(pl.multiple_of(d * g, g), g)],
                             ybuf_ref.at[chunk % 2, pl.ds(pl.multiple_of(j * g, g), g)], sem.at[chunk % 2])

    def for_used(chunk, action):
        def step(j, carry):
            @pl.when(j < nu_ref[chunk])
            def _():
                action(copy(chunk, j))
            return carry
        lax.fori_loop(0, MOE_GRANULES, step, 0, unroll=MOE_DMA_UNROLL)

    @pl.when(c == 0)
    def _():
        ybuf_ref[...] = jnp.zeros_like(ybuf_ref)
        for_used(c, lambda cp: cp.start())

    @pl.when(c + 1 < nc)
    def _():
        for_used(c + 1, lambda cp: cp.start())

    for_used(c, lambda cp: cp.wait())

    rio = lax.broadcasted_iota(I32, (MOE_CHUNK_ROWS, t), 0)
    w = (jnp.where(rio == pos_ref[0:1, :], gate_ref[0:1, :], 0.0)
         + jnp.where(rio == pos_ref[1:2, :], gate_ref[1:2, :], 0.0)).astype(BF16)
    y = lax.dot_general(w, ybuf_ref[slot], (((0,), (0,)), ((), ())), preferred_element_type=F32)
    out = h_ref[...] + y
    if final_norm:
        out = _rms(out, fg_ref[...])
    o_ref[...] = out


def _moe_combine(gdest, n_used, h, pos, gates, ys, final_gain, final_norm):
    n = h.shape[0]
    t = MOE_CHUNK
    nc = n // t
    return pl.pallas_call(
        functools.partial(_moe_combine_kernel, final_norm=final_norm),
        grid_spec=pltpu.PrefetchScalarGridSpec(
            num_scalar_prefetch=2,
            grid=(nc,),
            in_specs=[pl.BlockSpec((t, D_MODEL), lambda i, gd, nu: (i, 0)),
                      pl.BlockSpec((None, N_EXPERTS, t), lambda i, gd, nu: (i, 0, 0)),
                      pl.BlockSpec((None, N_EXPERTS, t), lambda i, gd, nu: (i, 0, 0)),
                      pl.BlockSpec(final_gain.shape, lambda i, gd, nu: (0, 0)),
                      pl.BlockSpec(memory_space=pl.ANY)],
            out_specs=pl.BlockSpec((t, D_MODEL), lambda i, gd, nu: (i, 0)),
            scratch_shapes=[pltpu.VMEM((2, MOE_CHUNK_ROWS, D_MODEL), BF16), pltpu.SemaphoreType.DMA((2,))]),
        out_shape=jax.ShapeDtypeStruct((n, D_MODEL), F32),
        compiler_params=_params("arbitrary"),
        name="moe_combine",
    )(gdest, n_used, h, pos, gates, final_gain, ys)


def _moe(a, wo, h, gain, router, wg, wu, wd, final_gain, final_norm):
    n = h.shape[0]
    wr = jnp.pad(router, ((0, 0), (0, LANES - N_EXPERTS)))
    wr_hi = wr.astype(BF16)
    wr_lo = (wr - wr_hi.astype(F32)).astype(BF16)
    h, xs, pos, gates, cnt = _moe_route(a, wo, h, gain, jnp.concatenate([wr_hi, wr_lo], axis=1))
    gdest, n_used, tile_expert, tiles_used, tail_start, tail_count = _moe_plan(cnt[:, :, 0], n)
    xg = _moe_scatter(gdest, n_used, tail_start, tail_count, tiles_used, xs, _moe_tiles(n) * MOE_ROW_TILE)
    ys = _moe_expert(tile_expert, tiles_used, xg, wg.astype(BF16), wu.astype(BF16), wd.astype(BF16))
    return _moe_combine(gdest, n_used, h, pos, gates, ys, final_gain, final_norm)


def _rot_cols(w):
    half = MLA_ROPE // 2
    return jnp.concatenate([-w[..., half:], w[..., :half]], axis=-1)


def _rope_table(seq):
    inv = 1.0 / (ROPE_THETA ** (jnp.arange(0, MLA_ROPE, 2, dtype=F32) / MLA_ROPE))
    ang = jnp.arange(seq, dtype=F32)[:, None] * inv[None, :]
    cos, sin = jnp.cos(ang), jnp.sin(ang)
    return jnp.concatenate([cos, cos, sin, sin], axis=-1)


def _mla_weights(w_dqkv, w_uq, w_uk, w_uv):
    lat = MLA_Q_RANK + MLA_KV_RANK
    rope = w_dqkv[:, lat:]
    wd = jnp.concatenate([w_dqkv[:, :lat], rope, _rot_cols(rope)], axis=1).astype(BF16)
    q_rope = w_uq[..., MLA_NOPE:]
    wuq = jnp.concatenate([w_uq[..., :MLA_NOPE], q_rope, _rot_cols(q_rope)], axis=-1)
    wuq = wuq.reshape(MLA_Q_RANK, MLA_HEADS * MLA_HEAD_LANES).astype(BF16)
    wuk = w_uk.reshape(MLA_KV_RANK, -1).astype(BF16)
    wuvt = w_uv.reshape(MLA_KV_RANK, -1).T.astype(BF16)
    return wd, wuq, wuk, wuvt


def _swa_weights(w_qkv):
    qw = SWA_Q_HEADS * SWA_HEAD_DIM
    kvw = SWA_KV_HEADS * SWA_HEAD_DIM
    dup = lambda w: jnp.concatenate([w.reshape(D_MODEL, SWA_KV_HEADS, 1, SWA_HEAD_DIM)] * 2, axis=2).reshape(D_MODEL, -1)
    wq = w_qkv[:, :qw] * (SWA_HEAD_DIM ** -0.5 * LOG2_E)
    return jnp.concatenate([wq, dup(w_qkv[:, qw:qw + kvw]), dup(w_qkv[:, qw + kvw:])], axis=1).astype(BF16)


def kernel(x, mla_norm, mla_w_dqkv, mla_q_norm, mla_w_uq, mla_kv_norm, mla_w_uk, mla_w_uv, mla_w_o, swa_norm, swa_w_qkv, swa_sink, swa_w_o, ffn_norm, ffn_w_gate, ffn_w_up, ffn_w_down, moe_norm, moe_router, moe_w_gate, moe_w_up, moe_w_down, final_norm):
    batch, seq, _ = x.shape
    h = x.reshape(batch * seq, D_MODEL)
    cs = _rope_table(seq)
    row = lambda v: v.reshape(1, -1).astype(F32)
    depth = 2 * mla_norm.shape[0]
    for layer in range(depth):
        j = layer // 2
        if layer % 2 == 0:
            wd, wuq, wuk, wuvt = _mla_weights(mla_w_dqkv[j], mla_w_uq[j], mla_w_uk[j], mla_w_uv[j])
            q, k, vt = _mla_proj(h, batch, seq, row(mla_norm[j]), wd, row(mla_q_norm[j]), row(mla_kv_norm[j]),
                                 wuq, wuk, wuvt, cs)
            a = _mla_attn(q, k, vt, batch, seq)
            h = _ffn(a, mla_w_o[j].astype(BF16), h, row(ffn_norm[j]), ffn_w_gate[j].astype(BF16),
                     ffn_w_up[j].astype(BF16), ffn_w_down[j].astype(BF16))
        else:
            qkv = _norm_matmul(h, row(swa_norm[j]), _swa_weights(swa_w_qkv[j]))
            a = _swa_attn(qkv, swa_sink[j].astype(F32), batch, seq)
            h = _moe(a, swa_w_o[j].astype(BF16), h, row(moe_norm[j]), moe_router[j], moe_w_gate[j], moe_w_up[j],
                     moe_w_down[j], row(final_norm), layer == depth - 1)
    return h.reshape(batch, seq, D_MODEL)
```

```python
import functools

import jax
import jax.numpy as jnp
from jax import lax
from jax.experimental import pallas as pl
from jax.experimental.pallas import tpu as pltpu

F32 = jnp.float32
BF16 = jnp.bfloat16
I32 = jnp.int32

D_MODEL = 1024
EPS = 1e-6
NEG_INF = -1e30
ROPE_THETA = 10000.0
LOG2_E = 1.4426950408889634

MLA_HEADS = 8
MLA_Q_RANK = 256
MLA_KV_RANK = 256
MLA_NOPE = 128
MLA_ROPE = 64
MLA_V = 128
MLA_QK = MLA_NOPE + MLA_ROPE
MLA_HEAD_LANES = 256

SWA_Q_HEADS = 16
SWA_KV_HEADS = 4
SWA_GROUP = 4
SWA_HEAD_DIM = 64
SWA_WINDOW = 128
SWA_BLOCK = 128
SWA_LOOKAHEAD = 2

FFN_DIM = 2816
N_EXPERTS = 8
EXPERT_DIM = 2048

LANES = 128
VMEM_LIMIT = 48 * 1024 * 1024

ROW_TILE = 512
ATTN_Q_TILE = 512
FFN_TILE = 1408
MOE_CHUNK = 512
MOE_GRANULE = 16
MOE_CHUNK_ROWS = 2 * MOE_CHUNK + N_EXPERTS * MOE_GRANULE
MOE_GRANULES = MOE_CHUNK_ROWS // MOE_GRANULE
MOE_DMA_UNROLL = 8
MOE_ROW_TILE = 512
MOE_F_TILE = 1024


def _params(*sem, flags=None):
    return pltpu.CompilerParams(dimension_semantics=sem, vmem_limit_bytes=VMEM_LIMIT, flags=flags)


def _rms(x, g):
    return x * lax.rsqrt(jnp.mean(x * x, axis=-1, keepdims=True) + EPS) * g


def _dot(a, b):
    return jnp.dot(a, b, preferred_element_type=F32)


def _dot_nt(a, b):
    return lax.dot_general(a, b, (((1,), (1,)), ((), ())), preferred_element_type=F32)


def _mla_proj_kernel(h_ref, g_ref, wd_ref, qn_ref, kvn_ref, wuq_ref, wuk_ref, wuvt_ref, cs_ref,
                     q_ref, k_ref, vt_ref):
    xn = _rms(h_ref[...], g_ref[...]).astype(BF16)
    down = _dot(xn, wd_ref[...])
    cq = _rms(down[:, :MLA_Q_RANK], qn_ref[...]).astype(BF16)
    ckv = _rms(down[:, MLA_Q_RANK:MLA_Q_RANK + MLA_KV_RANK], kvn_ref[...]).astype(BF16)
    cs = cs_ref[...]
    lane = lax.broadcasted_iota(I32, cs.shape, 1)

    def rope(a):
        p = a * cs
        return p + pltpu.roll(p, MLA_ROPE, 1)

    kr = jnp.where(lane < MLA_ROPE, rope(down[:, 2 * MLA_Q_RANK:]), 0.0)
    q = _dot(cq, wuq_ref[...]) * (MLA_QK ** -0.5 * LOG2_E)
    for h in range(MLA_HEADS):
        c = h * MLA_HEAD_LANES
        q_ref[:, c:c + LANES] = q[:, c:c + LANES].astype(BF16)
        q_ref[:, c + LANES:c + 2 * LANES] = rope(q[:, c + LANES:c + 2 * LANES]).astype(BF16)
    nk = MLA_HEADS * MLA_NOPE
    k_ref[:, :nk] = _dot(ckv, wuk_ref[...]).astype(BF16)
    k_ref[:, nk:] = kr.astype(BF16)
    vt_ref[...] = _dot_nt(wuvt_ref[...], ckv).astype(BF16)


def _mla_proj(h, batch, seq, gain, wd, qn, kvn, wuq, wuk, wuvt, cs):
    n = h.shape[0]
    t = ROW_TILE
    sb = seq // t
    full = lambda a: pl.BlockSpec(a.shape, lambda i: (0, 0))
    return pl.pallas_call(
        _mla_proj_kernel,
        grid=(n // t,),
        in_specs=[pl.BlockSpec((t, D_MODEL), lambda i: (i, 0)), full(gain), full(wd), full(qn), full(kvn),
                  full(wuq), full(wuk), full(wuvt), pl.BlockSpec((t, LANES), lambda i: (i % sb, 0))],
        out_specs=[pl.BlockSpec((t, MLA_HEADS * MLA_HEAD_LANES), lambda i: (i, 0)),
                   pl.BlockSpec((t, MLA_HEADS * MLA_NOPE + LANES), lambda i: (i, 0)),
                   pl.BlockSpec((None, MLA_HEADS * MLA_V, t), lambda i: (i // sb, 0, i % sb))],
        out_shape=[jax.ShapeDtypeStruct((n, MLA_HEADS * MLA_HEAD_LANES), BF16),
                   jax.ShapeDtypeStruct((n, MLA_HEADS * MLA_NOPE + LANES), BF16),
                   jax.ShapeDtypeStruct((batch, MLA_HEADS * MLA_V, seq), BF16)],
        compiler_params=_params("parallel"),
        name="mla_proj",
    )(h, gain, wd, qn, kvn, wuq, wuk, wuvt, cs)


def _mla_attn_kernel(q_ref, kn_ref, kr_ref, vt_ref, o_ref, kcat_ref):
    kcat_ref[:, :LANES] = kn_ref[...]
    kcat_ref[:, LANES:] = kr_ref[...]
    seq = q_ref.shape[0]

    tq = ATTN_Q_TILE
    n_tiles = seq // tq

    def scores(t):
        return _dot_nt(kcat_ref[...], q_ref[t * tq:(t + 1) * tq, :])

    def attend(st):
        pt = jnp.exp2(st - jnp.max(st, axis=0, keepdims=True))
        l = jnp.sum(pt, axis=0, keepdims=True)
        ot = _dot(vt_ref[...], pt.astype(BF16)) / l
        return ot.T.astype(o_ref.dtype)

    pending = scores(0)
    outs = []
    for t in range(n_tiles):
        st = pending
        if t + 1 < n_tiles:
            pending = scores(t + 1)
        outs.append(attend(st))
    o_ref[...] = jnp.concatenate(outs, axis=0)


def _mla_attn(q, k, vt, batch, seq):
    n = q.shape[0]
    return pl.pallas_call(
        _mla_attn_kernel,
        grid=(batch, MLA_HEADS),
        in_specs=[pl.BlockSpec((seq, MLA_HEAD_LANES), lambda b, h: (b, h)),
                  pl.BlockSpec((seq, LANES), lambda b, h: (b, h)),
                  pl.BlockSpec((seq, LANES), lambda b, h: (b, MLA_HEADS)),
                  pl.BlockSpec((None, MLA_V, seq), lambda b, h: (b, h, 0))],
        out_specs=pl.BlockSpec((seq, MLA_V), lambda b, h: (b, h)),
        out_shape=jax.ShapeDtypeStruct((n, MLA_HEADS * MLA_V), BF16),
        scratch_shapes=[pltpu.VMEM((seq, MLA_HEAD_LANES), BF16)],
        compiler_params=_params("parallel", "arbitrary"),
        name="mla_attn",
    )(q, k, k, vt)


def _ffn_kernel(a_ref, wo_ref, h_ref, g_ref, wg_ref, wu_ref, wd_ref, o_ref, xn_ref, acc_ref):
    f = pl.program_id(1)

    @pl.when(f == 0)
    def _():
        h1 = h_ref[...] + _dot(a_ref[...], wo_ref[...])
        xn_ref[...] = _rms(h1, g_ref[...]).astype(BF16)
        acc_ref[...] = h1

    xn = xn_ref[...]
    gate = _dot(xn, wg_ref[...])
    up = _dot(xn, wu_ref[...])
    act = (gate * jax.nn.sigmoid(gate) * up).astype(BF16)
    acc_ref[...] += _dot(act, wd_ref[...])

    @pl.when(f == pl.num_programs(1) - 1)
    def _():
        o_ref[...] = acc_ref[...]


def _ffn(a, wo, h, gain, wg, wu, wd):
    n = h.shape[0]
    t = ROW_TILE
    nf = FFN_DIM // FFN_TILE
    return pl.pallas_call(
        _ffn_kernel,
        grid=(n // t, nf),
        in_specs=[pl.BlockSpec((t, a.shape[1]), lambda i, f: (i, 0)),
                  pl.BlockSpec(wo.shape, lambda i, f: (0, 0)),
                  pl.BlockSpec((t, D_MODEL), lambda i, f: (i, 0)),
                  pl.BlockSpec(gain.shape, lambda i, f: (0, 0)),
                  pl.BlockSpec((D_MODEL, FFN_TILE), lambda i, f: (0, f)),
                  pl.BlockSpec((D_MODEL, FFN_TILE), lambda i, f: (0, f)),
                  pl.BlockSpec((FFN_TILE, D_MODEL), lambda i, f: (f, 0))],
        out_specs=pl.BlockSpec((t, D_MODEL), lambda i, f: (i, 0)),
        out_shape=jax.ShapeDtypeStruct((n, D_MODEL), F32),
        scratch_shapes=[pltpu.VMEM((t, D_MODEL), BF16), pltpu.VMEM((t, D_MODEL), F32)],
        compiler_params=_params("parallel", "arbitrary"),
        name="ffn",
    )(a, wo, h, gain, wg, wu, wd)


def _norm_matmul_kernel(h_ref, g_ref, w_ref, o_ref):
    o_ref[...] = _dot(_rms(h_ref[...], g_ref[...]).astype(BF16), w_ref[...]).astype(o_ref.dtype)


def _norm_matmul(h, gain, w):
    n = h.shape[0]
    t = ROW_TILE
    return pl.pallas_call(
        _norm_matmul_kernel,
        grid=(n // t,),
        in_specs=[pl.BlockSpec((t, D_MODEL), lambda i: (i, 0)),
                  pl.BlockSpec(gain.shape, lambda i: (0, 0)),
                  pl.BlockSpec(w.shape, lambda i: (0, 0))],
        out_specs=pl.BlockSpec((t, w.shape[1]), lambda i: (i, 0)),
        out_shape=jax.ShapeDtypeStruct((n, w.shape[1]), BF16),
        compiler_params=_params("parallel"),
        name="norm_matmul",
    )(h, gain, w)


def _alibi_slope(head):
    return 2.0 ** (-8.0 * (head + 1) / SWA_Q_HEADS)


def _swa_attn_kernel(sink_ref, q_ref, k_ref, v_ref, o_ref):
    i = pl.program_id(1)
    nb = pl.num_programs(1)
    blk = SWA_BLOCK
    pair_shape = (blk, 2 * blk)
    qpos = i * blk + lax.broadcasted_iota(I32, pair_shape, 0)
    col = lax.broadcasted_iota(I32, pair_shape, 1)
    kcol = jnp.where(col < blk, col, col - blk)
    lane = lax.broadcasted_iota(I32, (blk, LANES), 1)
    lo = lane < SWA_HEAD_DIM

    neg_dist, rows = [], []
    for j in (-1, 0, 1):
        kb = i + j
        kpos = kb * blk + kcol
        dist = jnp.abs(qpos - kpos)
        valid = (dist <= SWA_WINDOW) & (kb >= 0) & (kb < nb)
        neg_dist.append(jnp.where(valid, -dist.astype(F32), NEG_INF))
        rows.append(pl.multiple_of(jnp.clip(kb, 0, nb - 1) * blk, blk))

    def split(ref, g, r0):
        d = ref[pl.ds(r0, blk), g * LANES:(g + 1) * LANES]
        return jnp.concatenate([jnp.where(lo, d, 0), jnp.where(lo, 0, d)], axis=0)

    def scores(pair):
        h0 = 2 * pair
        qp = q_ref[:, pair * LANES:(pair + 1) * LANES]
        slope = jnp.where(col[0:1, :] < blk, _alibi_slope(h0) * LOG2_E, _alibi_slope(h0 + 1) * LOG2_E)
        return [_dot_nt(qp, split(k_ref, h0 // SWA_GROUP, rows[j])) + slope * neg_dist[j] for j in range(3)]

    def attend(pair, ss):
        h0 = 2 * pair
        ps, ls = [], []
        for half, head in ((0, h0), (1, h0 + 1)):
            sl = slice(half * blk, (half + 1) * blk)
            snk = sink_ref[head] * LOG2_E
            m = jnp.max(jnp.maximum(jnp.maximum(ss[0][:, sl], ss[1][:, sl]), ss[2][:, sl]), axis=-1, keepdims=True)
            m = jnp.maximum(m, snk)
            p3 = [jnp.exp2(ss[j][:, sl] - m) for j in range(3)]
            ls.append(jnp.sum(p3[0] + p3[1] + p3[2], axis=-1, keepdims=True) + jnp.exp2(snk - m))
            ps.append(p3)
        o = jnp.zeros((blk, LANES), F32)
        for j in range(3):
            pj = jnp.concatenate([ps[0][j], ps[1][j]], axis=1).astype(BF16)
            o = o + _dot(pj, split(v_ref, h0 // SWA_GROUP, rows[j]))
        return (o / jnp.where(lo, ls[0], ls[1])).astype(o_ref.dtype)

    n_pairs = SWA_Q_HEADS // 2
    pending = {p: scores(p) for p in range(min(SWA_LOOKAHEAD, n_pairs))}
    outs = []
    for p in range(n_pairs):
        if p + SWA_LOOKAHEAD < n_pairs:
            pending[p + SWA_LOOKAHEAD] = scores(p + SWA_LOOKAHEAD)
        outs.append(attend(p, pending.pop(p)))
    o_ref[...] = jnp.concatenate(outs, axis=1)


def _swa_attn(qkv, sink, batch, seq):
    n = qkv.shape[0]
    nb = seq // SWA_BLOCK
    qw = SWA_Q_HEADS * SWA_HEAD_DIM
    kvw = SWA_KV_HEADS * LANES
    kblk = qw // kvw
    return pl.pallas_call(
        _swa_attn_kernel,
        grid=(batch, nb),
        in_specs=[pl.BlockSpec(memory_space=pltpu.SMEM),
                  pl.BlockSpec((SWA_BLOCK, qw), lambda b, i: (b * nb + i, 0)),
                  pl.BlockSpec((seq, kvw), lambda b, i: (b, kblk)),
                  pl.BlockSpec((seq, kvw), lambda b, i: (b, kblk + 1))],
        out_specs=pl.BlockSpec((SWA_BLOCK, qw), lambda b, i: (b * nb + i, 0)),
        out_shape=jax.ShapeDtypeStruct((n, qw), BF16),
        compiler_params=_params("parallel", "arbitrary"),
        name="swa_attn",
    )(sink, qkv, qkv, qkv)


def _moe_route_kernel(a_ref, wo_ref, h_ref, g_ref, wr_ref, h1_ref, xs_ref, pos_ref, gate_ref, cnt_ref):
    t = MOE_CHUNK
    h1 = h_ref[...] + _dot(a_ref[...], wo_ref[...])
    h1_ref[...] = h1
    xn = _rms(h1, g_ref[...])
    x_hi = xn.astype(BF16)
    x_lo = (xn - x_hi.astype(F32)).astype(BF16)
    hi = _dot(x_hi, wr_ref[...])
    logits = hi[:, :LANES] + hi[:, LANES:] + _dot(x_lo, wr_ref[:, :LANES])
    lt = logits.T[:N_EXPERTS]
    eio = lax.broadcasted_iota(I32, lt.shape, 0)
    m1 = jnp.max(lt, axis=0, keepdims=True)
    i1 = jnp.min(jnp.where(lt == m1, eio, N_EXPERTS), axis=0, keepdims=True)
    l2 = jnp.where(eio == i1, -jnp.inf, lt)
    m2 = jnp.max(l2, axis=0, keepdims=True)
    i2 = jnp.min(jnp.where(l2 == m2, eio, N_EXPERTS), axis=0, keepdims=True)
    e21 = jnp.exp(m2 - m1)
    g1 = 1.0 / (1.0 + e21)
    g2 = e21 * g1

    sel1 = eio == i1
    sel2 = eio == i2
    oh = jnp.where(sel1 | sel2, 1.0, 0.0)
    tr = lax.broadcasted_iota(I32, (t, t), 0)
    tc = lax.broadcasted_iota(I32, (t, t), 1)
    before = jnp.where(tr < tc, 1.0, 0.0).astype(BF16)
    rank = _dot(oh.astype(BF16), before)
    cnt = jnp.sum(oh, axis=1, keepdims=True).astype(I32)
    padded = jnp.bitwise_and(cnt + (MOE_GRANULE - 1), -MOE_GRANULE)
    offs, run = [], jnp.zeros((1, 1), I32)
    for e in range(N_EXPERTS):
        offs.append(run)
        run = run + padded[e:e + 1, :]
    off = jnp.concatenate(offs, axis=0).astype(F32)
    slot = rank + off
    pos1 = jnp.sum(jnp.where(sel1, slot, 0.0), axis=0, keepdims=True).astype(I32)
    pos2 = jnp.sum(jnp.where(sel2, slot, 0.0), axis=0, keepdims=True).astype(I32)

    rio = lax.broadcasted_iota(I32, (MOE_CHUNK_ROWS, t), 0)
    perm = jnp.where(rio == pos1, 1.0, jnp.where(rio == pos2, 1.0, 0.0)).astype(BF16)
    xs_ref[...] = _dot(perm, x_hi).astype(BF16)

    zi = jnp.zeros((N_EXPERTS - 2, t), I32)
    pos_ref[...] = jnp.concatenate([pos1, pos2, zi], axis=0)
    gate_ref[...] = jnp.concatenate([g1, g2, zi.astype(F32)], axis=0)
    cnt_ref[...] = jnp.broadcast_to(padded, (N_EXPERTS, LANES))


def _moe_route(a, wo, h, gain, wr):
    n = h.shape[0]
    t = MOE_CHUNK
    nc = n // t
    return pl.pallas_call(
        _moe_route_kernel,
        grid=(nc,),
        in_specs=[pl.BlockSpec((t, a.shape[1]), lambda i: (i, 0)),
                  pl.BlockSpec(wo.shape, lambda i: (0, 0)),
                  pl.BlockSpec((t, D_MODEL), lambda i: (i, 0)),
                  pl.BlockSpec(gain.shape, lambda i: (0, 0)),
                  pl.BlockSpec(wr.shape, lambda i: (0, 0))],
        out_specs=[pl.BlockSpec((t, D_MODEL), lambda i: (i, 0)),
                   pl.BlockSpec((MOE_CHUNK_ROWS, D_MODEL), lambda i: (i, 0)),
                   pl.BlockSpec((None, N_EXPERTS, t), lambda i: (i, 0, 0)),
                   pl.BlockSpec((None, N_EXPERTS, t), lambda i: (i, 0, 0)),
                   pl.BlockSpec((None, N_EXPERTS, LANES), lambda i: (i, 0, 0))],
        out_shape=[jax.ShapeDtypeStruct((n, D_MODEL), F32),
                   jax.ShapeDtypeStruct((nc * MOE_CHUNK_ROWS, D_MODEL), BF16),
                   jax.ShapeDtypeStruct((nc, N_EXPERTS, t), I32),
                   jax.ShapeDtypeStruct((nc, N_EXPERTS, t), F32),
                   jax.ShapeDtypeStruct((nc, N_EXPERTS, LANES), I32)],
        compiler_params=_params("parallel"),
        name="moe_route",
    )(a, wo, h, gain, wr)


def _moe_plan(padded, n_tokens):
    nc = padded.shape[0]
    g = MOE_GRANULE
    loc_off = jnp.cumsum(padded, axis=1) - padded
    total = jnp.sum(padded, axis=0)
    total_pad = ((total + MOE_ROW_TILE - 1) // MOE_ROW_TILE) * MOE_ROW_TILE
    gbase = jnp.cumsum(total_pad) - total_pad
    coff = jnp.cumsum(padded, axis=0) - padded
    shift = gbase[None, :] + coff - loc_off
    step = shift[:, 1:] - shift[:, :-1]
    ends = (loc_off + padded)[:, :-1]
    row = jnp.arange(MOE_GRANULES, dtype=I32)[None, :, None] * g
    dest = row[:, :, 0] + shift[:, :1] + jnp.sum(jnp.where(ends[:, None, :] <= row, step[:, None, :], 0), axis=2)
    n_used = jnp.sum(padded, axis=1) // g
    valid = jnp.arange(MOE_GRANULES, dtype=I32)[None, :] < n_used[:, None]
    gdest = jnp.where(valid, dest // g, 0).astype(I32).reshape(-1)
    n_tiles = _moe_tiles(n_tokens)
    tile_row = jnp.arange(n_tiles, dtype=I32) * MOE_ROW_TILE
    tile_expert = jnp.minimum(jnp.sum(((gbase + total_pad)[None, :] <= tile_row[:, None]).astype(I32), axis=1),
                              N_EXPERTS - 1).astype(I32)
    tiles_used = ((gbase[-1] + total_pad[-1]) // MOE_ROW_TILE).astype(I32).reshape(1)
    tail_start = ((gbase + total) // g).astype(I32)
    tail_count = ((total_pad - total) // g).astype(I32)
    return gdest, n_used.astype(I32), tile_expert, tiles_used, tail_start, tail_count


def _moe_tiles(n_tokens):
    rows = (n_tokens // MOE_CHUNK) * MOE_CHUNK_ROWS + N_EXPERTS * MOE_ROW_TILE
    return -(-rows // MOE_ROW_TILE)


def _granule_copy(src, dst, sem):
    return pltpu.make_async_copy(src, dst, sem)


def _moe_scatter_kernel(gd_ref, nu_ref, ts_ref, tc_ref, nt_ref, xs_ref, out_ref, zero_ref, sem, zsem, tsem):
    g = MOE_GRANULE
    c = pl.program_id(0)
    n_tiles = out_ref.shape[0] // MOE_ROW_TILE

    def zero_copy(e, k):
        return _granule_copy(zero_ref.at[pl.ds(0, g)],
                             out_ref.at[pl.ds(pl.multiple_of((ts_ref[e] + k) * g, g), g)], zsem)

    def for_tail(action):
        for e in range(N_EXPERTS):
            def step(k, carry):
                @pl.when(k < tc_ref[e])
                def _():
                    action(zero_copy(e, k))
                return carry
            lax.fori_loop(0, MOE_ROW_TILE // g, step, 0, unroll=MOE_DMA_UNROLL)

    def for_unused_tiles(action):
        def step(i, carry):
            @pl.when(i >= nt_ref[0])
            def _():
                action(_granule_copy(
                    zero_ref, out_ref.at[pl.ds(pl.multiple_of(i * MOE_ROW_TILE, MOE_ROW_TILE), MOE_ROW_TILE)], tsem))
            return carry
        lax.fori_loop(0, n_tiles, step, 0)

    def copy(j):
        d = gd_ref[c * MOE_GRANULES + j]
        return _granule_copy(xs_ref.at[pl.ds(pl.multiple_of(j * g, g), g)],
                             out_ref.at[pl.ds(pl.multiple_of(d * g, g), g)], sem)

    def for_used(action):
        def step(j, carry):
            @pl.when(j < nu_ref[c])
            def _():
                action(copy(j))
            return carry
        lax.fori_loop(0, MOE_GRANULES, step, 0, unroll=MOE_DMA_UNROLL)

    for_used(lambda cp: cp.start())

    @pl.when(c == 0)
    def _():
        zero_ref[...] = jnp.zeros_like(zero_ref)
        for_tail(lambda cp: cp.start())
        for_unused_tiles(lambda cp: cp.start())
        for_tail(lambda cp: cp.wait())
        for_unused_tiles(lambda cp: cp.wait())

    for_used(lambda cp: cp.wait())


def _moe_scatter(gdest, n_used, tail_start, tail_count, tiles_used, xs, n_rows):
    return pl.pallas_call(
        _moe_scatter_kernel,
        grid_spec=pltpu.PrefetchScalarGridSpec(
            num_scalar_prefetch=5,
            grid=(n_used.shape[0],),
            in_specs=[pl.BlockSpec((MOE_CHUNK_ROWS, D_MODEL), lambda i, *_: (i, 0))],
            out_specs=pl.BlockSpec(memory_space=pl.ANY),
            scratch_shapes=[pltpu.VMEM((MOE_ROW_TILE, D_MODEL), BF16), pltpu.SemaphoreType.DMA(()),
                            pltpu.SemaphoreType.DMA(()), pltpu.SemaphoreType.DMA(())]),
        out_shape=jax.ShapeDtypeStruct((n_rows, D_MODEL), BF16),
        compiler_params=_params("arbitrary"),
        name="moe_scatter",
    )(gdest, n_used, tail_start, tail_count, tiles_used, xs)


def _moe_expert_kernel(te_ref, nt_ref, x_ref, wg_ref, wu_ref, wd_ref, o_ref, acc_ref):
    i = pl.program_id(0)
    f = pl.program_id(1)
    last = pl.num_programs(1) - 1
    live = i < nt_ref[0]

    @pl.when(live)
    def _():
        @pl.when(f == 0)
        def _():
            acc_ref[...] = jnp.zeros_like(acc_ref)

        x = x_ref[...]
        gate = _dot(x, wg_ref[...])
        up = _dot(x, wu_ref[...])
        act = (gate * jax.nn.sigmoid(gate) * up).astype(BF16)
        acc_ref[...] += _dot(act, wd_ref[...])

        @pl.when(f == last)
        def _():
            o_ref[...] = acc_ref[...].astype(o_ref.dtype)

    @pl.when(jnp.logical_not(live) & (f == last))
    def _():
        o_ref[...] = jnp.zeros_like(o_ref)


def _moe_expert(tile_expert, tiles_used, xg, wg, wu, wd):
    n_rows = xg.shape[0]
    nt = n_rows // MOE_ROW_TILE
    nf = EXPERT_DIM // MOE_F_TILE

    def tile(i, nt_ref):
        return jnp.minimum(i, nt_ref[0] - 1)

    def fcol(i, f, nt_ref):
        return jnp.where(i < nt_ref[0], f, nf - 1)

    return pl.pallas_call(
        _moe_expert_kernel,
        grid_spec=pltpu.PrefetchScalarGridSpec(
            num_scalar_prefetch=2,
            grid=(nt, nf),
            in_specs=[pl.BlockSpec((MOE_ROW_TILE, D_MODEL), lambda i, f, te, ntr: (tile(i, ntr), 0)),
                      pl.BlockSpec((None, D_MODEL, MOE_F_TILE),
                                   lambda i, f, te, ntr: (te[tile(i, ntr)], 0, fcol(i, f, ntr))),
                      pl.BlockSpec((None, D_MODEL, MOE_F_TILE),
                                   lambda i, f, te, ntr: (te[tile(i, ntr)], 0, fcol(i, f, ntr))),
                      pl.BlockSpec((None, MOE_F_TILE, D_MODEL),
                                   lambda i, f, te, ntr: (te[tile(i, ntr)], fcol(i, f, ntr), 0))],
            out_specs=pl.BlockSpec((MOE_ROW_TILE, D_MODEL), lambda i, f, te, ntr: (i, 0)),
            scratch_shapes=[pltpu.VMEM((MOE_ROW_TILE, D_MODEL), F32)]),
        out_shape=jax.ShapeDtypeStruct((n_rows, D_MODEL), BF16),
        compiler_params=_params("arbitrary", "arbitrary"),
        name="moe_expert",
    )(tile_expert, tiles_used, xg, wg, wu, wd)


def _moe_combine_kernel(gd_ref, nu_ref, h_ref, pos_ref, gate_ref, fg_ref, ys_ref, o_ref, ybuf_ref, sem,
                        *, final_norm):
    c = pl.program_id(0)
    nc = pl.num_programs(0)
    g = MOE_GRANULE
    t = MOE_CHUNK
    slot = c % 2

    def copy(chunk, j):
        d = gd_ref[chunk * MOE_GRANULES + j]
        return _granule_copy(ys_ref.at[pl.ds(pl.multiple_of(d * g, g), g)],
                             ybuf_ref.at[chunk % 2, pl.ds(pl.multiple_of(j * g, g), g)], sem.at[chunk % 2])

    def for_used(chunk, action):
        def step(j, carry):
            @pl.when(j < nu_ref[chunk])
            def _():
                action(copy(chunk, j))
            return carry
        lax.fori_loop(0, MOE_GRANULES, step, 0, unroll=MOE_DMA_UNROLL)

    @pl.when(c == 0)
    def _():
        ybuf_ref[...] = jnp.zeros_like(ybuf_ref)
        for_used(c, lambda cp: cp.start())

    @pl.when(c + 1 < nc)
    def _():
        for_used(c + 1, lambda cp: cp.start())

    for_used(c, lambda cp: cp.wait())

    rio = lax.broadcasted_iota(I32, (MOE_CHUNK_ROWS, t), 0)
    w = (jnp.where(rio == pos_ref[0:1, :], gate_ref[0:1, :], 0.0)
         + jnp.where(rio == pos_ref[1:2, :], gate_ref[1:2, :], 0.0)).astype(BF16)
    y = lax.dot_general(w, ybuf_ref[slot], (((0,), (0,)), ((), ())), preferred_element_type=F32)
    out = h_ref[...] + y
    if final_norm:
        out = _rms(out, fg_ref[...])
    o_ref[...] = out


def _moe_combine(gdest, n_used, h, pos, gates, ys, final_gain, final_norm):
    n = h.shape[0]
    t = MOE_CHUNK
    nc = n // t
    return pl.pallas_call(
        functools.partial(_moe_combine_kernel, final_norm=final_norm),
        grid_spec=pltpu.PrefetchScalarGridSpec(
            num_scalar_prefetch=2,
            grid=(nc,),
            in_specs=[pl.BlockSpec((t, D_MODEL), lambda i, gd, nu: (i, 0)),
                      pl.BlockSpec((None, N_EXPERTS, t), lambda i, gd, nu: (i, 0, 0)),
                      pl.BlockSpec((None, N_EXPERTS, t), lambda i, gd, nu: (i, 0, 0)),
                      pl.BlockSpec(final_gain.shape, lambda i, gd, nu: (0, 0)),
                      pl.BlockSpec(memory_space=pl.ANY)],
            out_specs=pl.BlockSpec((t, D_MODEL), lambda i, gd, nu: (i, 0)),
            scratch_shapes=[pltpu.VMEM((2, MOE_CHUNK_ROWS, D_MODEL), BF16), pltpu.SemaphoreType.DMA((2,))]),
        out_shape=jax.ShapeDtypeStruct((n, D_MODEL), F32),
        compiler_params=_params("arbitrary"),
        name="moe_combine",
    )(gdest, n_used, h, pos, gates, final_gain, ys)


def _moe(a, wo, h, gain, router, wg, wu, wd, final_gain, final_norm):
    n = h.shape[0]
    wr = jnp.pad(router, ((0, 0), (0, LANES - N_EXPERTS)))
    wr_hi = wr.astype(BF16)
    wr_lo = (wr - wr_hi.astype(F32)).astype(BF16)
    h, xs, pos, gates, cnt = _moe_route(a, wo, h, gain, jnp.concatenate([wr_hi, wr_lo], axis=1))
    gdest, n_used, tile_expert, tiles_used, tail_start, tail_count = _moe_plan(cnt[:, :, 0], n)
    xg = _moe_scatter(gdest, n_used, tail_start, tail_count, tiles_used, xs, _moe_tiles(n) * MOE_ROW_TILE)
    ys = _moe_expert(tile_expert, tiles_used, xg, wg.astype(BF16), wu.astype(BF16), wd.astype(BF16))
    return _moe_combine(gdest, n_used, h, pos, gates, ys, final_gain, final_norm)


def _rot_cols(w):
    half = MLA_ROPE // 2
    return jnp.concatenate([-w[..., half:], w[..., :half]], axis=-1)


def _rope_table(seq):
    inv = 1.0 / (ROPE_THETA ** (jnp.arange(0, MLA_ROPE, 2, dtype=F32) / MLA_ROPE))
    ang = jnp.arange(seq, dtype=F32)[:, None] * inv[None, :]
    cos, sin = jnp.cos(ang), jnp.sin(ang)
    return jnp.concatenate([cos, cos, sin, sin], axis=-1)


def _mla_weights(w_dqkv, w_uq, w_uk, w_uv):
    lat = MLA_Q_RANK + MLA_KV_RANK
    rope = w_dqkv[:, lat:]
    wd = jnp.concatenate([w_dqkv[:, :lat], rope, _rot_cols(rope)], axis=1).astype(BF16)
    q_rope = w_uq[..., MLA_NOPE:]
    wuq = jnp.concatenate([w_uq[..., :MLA_NOPE], q_rope, _rot_cols(q_rope)], axis=-1)
    wuq = wuq.reshape(MLA_Q_RANK, MLA_HEADS * MLA_HEAD_LANES).astype(BF16)
    wuk = w_uk.reshape(MLA_KV_RANK, -1).astype(BF16)
    wuvt = w_uv.reshape(MLA_KV_RANK, -1).T.astype(BF16)
    return wd, wuq, wuk, wuvt


def _swa_weights(w_qkv):
    qw = SWA_Q_HEADS * SWA_HEAD_DIM
    kvw = SWA_KV_HEADS * SWA_HEAD_DIM
    dup = lambda w: jnp.concatenate([w.reshape(D_MODEL, SWA_KV_HEADS, 1, SWA_HEAD_DIM)] * 2, axis=2).reshape(D_MODEL, -1)
    wq = w_qkv[:, :qw] * (SWA_HEAD_DIM ** -0.5 * LOG2_E)
    return jnp.concatenate([wq, dup(w_qkv[:, qw:qw + kvw]), dup(w_qkv[:, qw + kvw:])], axis=1).astype(BF16)


def kernel(x, mla_norm, mla_w_dqkv, mla_q_norm, mla_w_uq, mla_kv_norm, mla_w_uk, mla_w_uv, mla_w_o, swa_norm, swa_w_qkv, swa_sink, swa_w_o, ffn_norm, ffn_w_gate, ffn_w_up, ffn_w_down, moe_norm, moe_router, moe_w_gate, moe_w_up, moe_w_down, final_norm):
    batch, seq, _ = x.shape
    h = x.reshape(batch * seq, D_MODEL)
    cs = _rope_table(seq)
    row = lambda v: v.reshape(1, -1).astype(F32)
    depth = 2 * mla_norm.shape[0]
    for layer in range(depth):
        j = layer // 2
        if layer % 2 == 0:
            wd, wuq, wuk, wuvt = _mla_weights(mla_w_dqkv[j], mla_w_uq[j], mla_w_uk[j], mla_w_uv[j])
            q, k, vt = _mla_proj(h, batch, seq, row(mla_norm[j]), wd, row(mla_q_norm[j]), row(mla_kv_norm[j]),
                                 wuq, wuk, wuvt, cs)
            a = _mla_attn(q, k, vt, batch, seq)
            h = _ffn(a, mla_w_o[j].astype(BF16), h, row(ffn_norm[j]), ffn_w_gate[j].astype(BF16),
                     ffn_w_up[j].astype(BF16), ffn_w_down[j].astype(BF16))
        else:
            qkv = _norm_matmul(h, row(swa_norm[j]), _swa_weights(swa_w_qkv[j]))
            a = _swa_attn(qkv, swa_sink[j].astype(F32), batch, seq)
            h = _moe(a, swa_w_o[j].astype(BF16), h, row(moe_norm[j]), moe_router[j], moe_w_gate[j], moe_w_up[j],
                     moe_w_down[j], row(final_norm), layer == depth - 1)
    return h.reshape(batch, seq, D_MODEL)
```

```python
import functools

import jax
import jax.numpy as jnp
from jax import lax
from jax.experimental import pallas as pl
from jax.experimental.pallas import tpu as pltpu

F32 = jnp.float32
BF16 = jnp.bfloat16
I32 = jnp.int32

D_MODEL = 1024
EPS = 1e-6
NEG_INF = -1e30
ROPE_THETA = 10000.0
LOG2_E = 1.4426950408889634

MLA_HEADS = 8
MLA_Q_RANK = 256
MLA_KV_RANK = 256
MLA_NOPE = 128
MLA_ROPE = 64
MLA_V = 128
MLA_QK = MLA_NOPE + MLA_ROPE
MLA_HEAD_LANES = 256

SWA_Q_HEADS = 16
SWA_KV_HEADS = 4
SWA_GROUP = 4
SWA_HEAD_DIM = 64
SWA_WINDOW = 128
SWA_BLOCK = 128
SWA_LOOKAHEAD = 2

FFN_DIM = 2816
N_EXPERTS = 8
EXPERT_DIM = 2048

LANES = 128
VMEM_LIMIT = 48 * 1024 * 1024

ROW_TILE = 512
ATTN_Q_TILE = 512
MOE_CHUNK = 512
MOE_GRANULE = 16
MOE_CHUNK_ROWS = 2 * MOE_CHUNK + N_EXPERTS * MOE_GRANULE
MOE_GRANULES = MOE_CHUNK_ROWS // MOE_GRANULE
MOE_DMA_UNROLL = 8
MOE_ROW_TILE = 512
MOE_F_TILE = 2048


def _params(*sem, flags=None):
    return pltpu.CompilerParams(dimension_semantics=sem, vmem_limit_bytes=VMEM_LIMIT, flags=flags)


def _rms(x, g):
    return x * lax.rsqrt(jnp.mean(x * x, axis=-1, keepdims=True) + EPS) * g


def _dot(a, b):
    return jnp.dot(a, b, preferred_element_type=F32)


def _dot_nt(a, b):
    return lax.dot_general(a, b, (((1,), (1,)), ((), ())), preferred_element_type=F32)


def _mla_proj_kernel(h_ref, g_ref, wd_ref, qn_ref, kvn_ref, wuq_ref, wuk_ref, wuvt_ref, cs_ref,
                     q_ref, k_ref, vt_ref):
    xn = _rms(h_ref[...], g_ref[...]).astype(BF16)
    down = _dot(xn, wd_ref[...])
    cq = _rms(down[:, :MLA_Q_RANK], qn_ref[...]).astype(BF16)
    ckv = _rms(down[:, MLA_Q_RANK:MLA_Q_RANK + MLA_KV_RANK], kvn_ref[...]).astype(BF16)
    cs = cs_ref[...]
    lane = lax.broadcasted_iota(I32, cs.shape, 1)

    def rope(a):
        p = a * cs
        return p + pltpu.roll(p, MLA_ROPE, 1)

    kr = jnp.where(lane < MLA_ROPE, rope(down[:, 2 * MLA_Q_RANK:]), 0.0)
    q = _dot(cq, wuq_ref[...]) * (MLA_QK ** -0.5 * LOG2_E)
    for h in range(MLA_HEADS):
        c = h * MLA_HEAD_LANES
        q_ref[:, c:c + LANES] = q[:, c:c + LANES].astype(BF16)
        q_ref[:, c + LANES:c + 2 * LANES] = rope(q[:, c + LANES:c + 2 * LANES]).astype(BF16)
    nk = MLA_HEADS * MLA_NOPE
    k_ref[:, :nk] = _dot(ckv, wuk_ref[...]).astype(BF16)
    k_ref[:, nk:] = kr.astype(BF16)
    vt_ref[...] = _dot_nt(wuvt_ref[...], ckv).astype(BF16)


def _mla_proj(h, batch, seq, gain, wd, qn, kvn, wuq, wuk, wuvt, cs):
    n = h.shape[0]
    t = ROW_TILE
    sb = seq // t
    full = lambda a: pl.BlockSpec(a.shape, lambda i: (0, 0))
    return pl.pallas_call(
        _mla_proj_kernel,
        grid=(n // t,),
        in_specs=[pl.BlockSpec((t, D_MODEL), lambda i: (i, 0)), full(gain), full(wd), full(qn), full(kvn),
                  full(wuq), full(wuk), full(wuvt), pl.BlockSpec((t, LANES), lambda i: (i % sb, 0))],
        out_specs=[pl.BlockSpec((t, MLA_HEADS * MLA_HEAD_LANES), lambda i: (i, 0)),
                   pl.BlockSpec((t, MLA_HEADS * MLA_NOPE + LANES), lambda i: (i, 0)),
                   pl.BlockSpec((None, MLA_HEADS * MLA_V, t), lambda i: (i // sb, 0, i % sb))],
        out_shape=[jax.ShapeDtypeStruct((n, MLA_HEADS * MLA_HEAD_LANES), BF16),
                   jax.ShapeDtypeStruct((n, MLA_HEADS * MLA_NOPE + LANES), BF16),
                   jax.ShapeDtypeStruct((batch, MLA_HEADS * MLA_V, seq), BF16)],
        compiler_params=_params("parallel"),
        name="mla_proj",
    )(h, gain, wd, qn, kvn, wuq, wuk, wuvt, cs)


def _mla_attn_kernel(q_ref, kn_ref, kr_ref, vt_ref, o_ref, kcat_ref):
    kcat_ref[:, :LANES] = kn_ref[...]
    kcat_ref[:, LANES:] = kr_ref[...]
    seq = q_ref.shape[0]

    tq = ATTN_Q_TILE
    n_tiles = seq // tq

    def scores(t):
        return _dot_nt(kcat_ref[...], q_ref[t * tq:(t + 1) * tq, :])

    def attend(st):
        pt = jnp.exp2(st - jnp.max(st, axis=0, keepdims=True))
        l = jnp.sum(pt, axis=0, keepdims=True)
        ot = _dot(vt_ref[...], pt.astype(BF16)) / l
        return ot.T.astype(o_ref.dtype)

    pending = scores(0)
    outs = []
    for t in range(n_tiles):
        st = pending
        if t + 1 < n_tiles:
            pending = scores(t + 1)
        outs.append(attend(st))
    o_ref[...] = jnp.concatenate(outs, axis=0)


def _mla_attn(q, k, vt, batch, seq):
    n = q.shape[0]
    return pl.pallas_call(
        _mla_attn_kernel,
        grid=(batch, MLA_HEADS),
        in_specs=[pl.BlockSpec((seq, MLA_HEAD_LANES), lambda b, h: (b, h)),
                  pl.BlockSpec((seq, LANES), lambda b, h: (b, h)),
                  pl.BlockSpec((seq, LANES), lambda b, h: (b, MLA_HEADS)),
                  pl.BlockSpec((None, MLA_V, seq), lambda b, h: (b, h, 0))],
        out_specs=pl.BlockSpec((seq, MLA_V), lambda b, h: (b, h)),
        out_shape=jax.ShapeDtypeStruct((n, MLA_HEADS * MLA_V), BF16),
        scratch_shapes=[pltpu.VMEM((seq, MLA_HEAD_LANES), BF16)],
        compiler_params=_params("parallel", "arbitrary"),
        name="mla_attn",
    )(q, k, k, vt)


def _ffn_kernel(a_ref, wo_ref, h_ref, g_ref, wg_ref, wu_ref, wd_ref, o_ref):
    h1 = h_ref[...] + _dot(a_ref[...], wo_ref[...])
    xn = _rms(h1, g_ref[...]).astype(BF16)
    gate = _dot(xn, wg_ref[...])
    up = _dot(xn, wu_ref[...])
    act = (gate * jax.nn.sigmoid(gate) * up).astype(BF16)
    o_ref[...] = h1 + _dot(act, wd_ref[...])


def _resident(a):
    return pl.BlockSpec(a.shape, lambda i: (0,) * a.ndim, pipeline_mode=pl.Buffered(1))


def _ffn(a, wo, h, gain, wg, wu, wd):
    n = h.shape[0]
    t = ROW_TILE
    return pl.pallas_call(
        _ffn_kernel,
        grid=(n // t,),
        in_specs=[pl.BlockSpec((t, a.shape[1]), lambda i: (i, 0)), _resident(wo),
                  pl.BlockSpec((t, D_MODEL), lambda i: (i, 0)), _resident(gain),
                  _resident(wg), _resident(wu), _resident(wd)],
        out_specs=pl.BlockSpec((t, D_MODEL), lambda i: (i, 0)),
        out_shape=jax.ShapeDtypeStruct((n, D_MODEL), F32),
        compiler_params=_params("parallel"),
        name="ffn",
    )(a, wo, h, gain, wg, wu, wd)


def _norm_matmul_kernel(h_ref, g_ref, w_ref, o_ref):
    o_ref[...] = _dot(_rms(h_ref[...], g_ref[...]).astype(BF16), w_ref[...]).astype(o_ref.dtype)


def _norm_matmul(h, gain, w):
    n = h.shape[0]
    t = ROW_TILE
    return pl.pallas_call(
        _norm_matmul_kernel,
        grid=(n // t,),
        in_specs=[pl.BlockSpec((t, D_MODEL), lambda i: (i, 0)),
                  pl.BlockSpec(gain.shape, lambda i: (0, 0)),
                  pl.BlockSpec(w.shape, lambda i: (0, 0))],
        out_specs=pl.BlockSpec((t, w.shape[1]), lambda i: (i, 0)),
        out_shape=jax.ShapeDtypeStruct((n, w.shape[1]), BF16),
        compiler_params=_params("parallel"),
        name="norm_matmul",
    )(h, gain, w)


def _alibi_slope(head):
    return 2.0 ** (-8.0 * (head + 1) / SWA_Q_HEADS)


def _swa_attn_kernel(sink_ref, q_ref, k_ref, v_ref, o_ref):
    i = pl.program_id(1)
    nb = pl.num_programs(1)
    blk = SWA_BLOCK
    pair_shape = (blk, 2 * blk)
    qpos = i * blk + lax.broadcasted_iota(I32, pair_shape, 0)
    col = lax.broadcasted_iota(I32, pair_shape, 1)
    kcol = jnp.where(col < blk, col, col - blk)
    lane = lax.broadcasted_iota(I32, (blk, LANES), 1)
    lo = lane < SWA_HEAD_DIM

    neg_dist, rows = [], []
    for j in (-1, 0, 1):
        kb = i + j
        kpos = kb * blk + kcol
        dist = jnp.abs(qpos - kpos)
        valid = (dist <= SWA_WINDOW) & (kb >= 0) & (kb < nb)
        neg_dist.append(jnp.where(valid, -dist.astype(F32), NEG_INF))
        rows.append(pl.multiple_of(jnp.clip(kb, 0, nb - 1) * blk, blk))

    def split(ref, g, r0):
        d = ref[pl.ds(r0, blk), g * LANES:(g + 1) * LANES]
        return jnp.concatenate([jnp.where(lo, d, 0), jnp.where(lo, 0, d)], axis=0)

    def scores(pair):
        h0 = 2 * pair
        qp = q_ref[:, pair * LANES:(pair + 1) * LANES]
        slope = jnp.where(col[0:1, :] < blk, _alibi_slope(h0) * LOG2_E, _alibi_slope(h0 + 1) * LOG2_E)
        return [_dot_nt(qp, split(k_ref, h0 // SWA_GROUP, rows[j])) + slope * neg_dist[j] for j in range(3)]

    def attend(pair, ss):
        h0 = 2 * pair
        ps, ls = [], []
        for half, head in ((0, h0), (1, h0 + 1)):
            sl = slice(half * blk, (half + 1) * blk)
            snk = sink_ref[head] * LOG2_E
            m = jnp.max(jnp.maximum(jnp.maximum(ss[0][:, sl], ss[1][:, sl]), ss[2][:, sl]), axis=-1, keepdims=True)
            m = jnp.maximum(m, snk)
            p3 = [jnp.exp2(ss[j][:, sl] - m) for j in range(3)]
            ls.append(jnp.sum(p3[0] + p3[1] + p3[2], axis=-1, keepdims=True) + jnp.exp2(snk - m))
            ps.append(p3)
        o = jnp.zeros((blk, LANES), F32)
        for j in range(3):
            pj = jnp.concatenate([ps[0][j], ps[1][j]], axis=1).astype(BF16)
            o = o + _dot(pj, split(v_ref, h0 // SWA_GROUP, rows[j]))
        return (o / jnp.where(lo, ls[0], ls[1])).astype(o_ref.dtype)

    n_pairs = SWA_Q_HEADS // 2
    pending = {p: scores(p) for p in range(min(SWA_LOOKAHEAD, n_pairs))}
    outs = []
    for p in range(n_pairs):
        if p + SWA_LOOKAHEAD < n_pairs:
            pending[p + SWA_LOOKAHEAD] = scores(p + SWA_LOOKAHEAD)
        outs.append(attend(p, pending.pop(p)))
    o_ref[...] = jnp.concatenate(outs, axis=1)


def _swa_attn(qkv, sink, batch, seq):
    n = qkv.shape[0]
    nb = seq // SWA_BLOCK
    qw = SWA_Q_HEADS * SWA_HEAD_DIM
    kvw = SWA_KV_HEADS * LANES
    kblk = qw // kvw
    return pl.pallas_call(
        _swa_attn_kernel,
        grid=(batch, nb),
        in_specs=[pl.BlockSpec(memory_space=pltpu.SMEM),
                  pl.BlockSpec((SWA_BLOCK, qw), lambda b, i: (b * nb + i, 0)),
                  pl.BlockSpec((seq, kvw), lambda b, i: (b, kblk)),
                  pl.BlockSpec((seq, kvw), lambda b, i: (b, kblk + 1))],
        out_specs=pl.BlockSpec((SWA_BLOCK, qw), lambda b, i: (b * nb + i, 0)),
        out_shape=jax.ShapeDtypeStruct((n, qw), BF16),
        compiler_params=_params("parallel", "arbitrary"),
        name="swa_attn",
    )(sink, qkv, qkv, qkv)


def _moe_route_kernel(a_ref, wo_ref, h_ref, g_ref, wr_ref, h1_ref, xs_ref, pos_ref, gate_ref, cnt_ref):
    t = MOE_CHUNK
    h1 = h_ref[...] + _dot(a_ref[...], wo_ref[...])
    h1_ref[...] = h1
    xn = _rms(h1, g_ref[...])
    x_hi = xn.astype(BF16)
    x_lo = (xn - x_hi.astype(F32)).astype(BF16)
    hi = _dot(x_hi, wr_ref[...])
    logits = hi[:, :LANES] + hi[:, LANES:] + _dot(x_lo, wr_ref[:, :LANES])
    lt = logits.T[:N_EXPERTS]
    eio = lax.broadcasted_iota(I32, lt.shape, 0)
    m1 = jnp.max(lt, axis=0, keepdims=True)
    i1 = jnp.min(jnp.where(lt == m1, eio, N_EXPERTS), axis=0, keepdims=True)
    l2 = jnp.where(eio == i1, -jnp.inf, lt)
    m2 = jnp.max(l2, axis=0, keepdims=True)
    i2 = jnp.min(jnp.where(l2 == m2, eio, N_EXPERTS), axis=0, keepdims=True)
    e21 = jnp.exp(m2 - m1)
    g1 = 1.0 / (1.0 + e21)
    g2 = e21 * g1

    sel1 = eio == i1
    sel2 = eio == i2
    oh = jnp.where(sel1 | sel2, 1.0, 0.0)
    tr = lax.broadcasted_iota(I32, (t, t), 0)
    tc = lax.broadcasted_iota(I32, (t, t), 1)
    before = jnp.where(tr < tc, 1.0, 0.0).astype(BF16)
    rank = _dot(oh.astype(BF16), before)
    cnt = jnp.sum(oh, axis=1, keepdims=True).astype(I32)
    padded = jnp.bitwise_and(cnt + (MOE_GRANULE - 1), -MOE_GRANULE)
    offs, run = [], jnp.zeros((1, 1), I32)
    for e in range(N_EXPERTS):
        offs.append(run)
        run = run + padded[e:e + 1, :]
    off = jnp.concatenate(offs, axis=0).astype(F32)
    slot = rank + off
    pos1 = jnp.sum(jnp.where(sel1, slot, 0.0), axis=0, keepdims=True).astype(I32)
    pos2 = jnp.sum(jnp.where(sel2, slot, 0.0), axis=0, keepdims=True).astype(I32)

    rio = lax.broadcasted_iota(I32, (MOE_CHUNK_ROWS, t), 0)
    perm = jnp.where(rio == pos1, 1.0, jnp.where(rio == pos2, 1.0, 0.0)).astype(BF16)
    xs_ref[...] = _dot(perm, x_hi).astype(BF16)

    zi = jnp.zeros((N_EXPERTS - 2, t), I32)
    pos_ref[...] = jnp.concatenate([pos1, pos2, zi], axis=0)
    gate_ref[...] = jnp.concatenate([g1, g2, zi.astype(F32)], axis=0)
    cnt_ref[...] = jnp.broadcast_to(padded, (N_EXPERTS, LANES))


def _moe_route(a, wo, h, gain, wr):
    n = h.shape[0]
    t = MOE_CHUNK
    nc = n // t
    return pl.pallas_call(
        _moe_route_kernel,
        grid=(nc,),
        in_specs=[pl.BlockSpec((t, a.shape[1]), lambda i: (i, 0)),
                  pl.BlockSpec(wo.shape, lambda i: (0, 0)),
                  pl.BlockSpec((t, D_MODEL), lambda i: (i, 0)),
                  pl.BlockSpec(gain.shape, lambda i: (0, 0)),
                  pl.BlockSpec(wr.shape, lambda i: (0, 0))],
        out_specs=[pl.BlockSpec((t, D_MODEL), lambda i: (i, 0)),
                   pl.BlockSpec((MOE_CHUNK_ROWS, D_MODEL), lambda i: (i, 0)),
                   pl.BlockSpec((None, N_EXPERTS, t), lambda i: (i, 0, 0)),
                   pl.BlockSpec((None, N_EXPERTS, t), lambda i: (i, 0, 0)),
                   pl.BlockSpec((None, N_EXPERTS, LANES), lambda i: (i, 0, 0))],
        out_shape=[jax.ShapeDtypeStruct((n, D_MODEL), F32),
                   jax.ShapeDtypeStruct((nc * MOE_CHUNK_ROWS, D_MODEL), BF16),
                   jax.ShapeDtypeStruct((nc, N_EXPERTS, t), I32),
                   jax.ShapeDtypeStruct((nc, N_EXPERTS, t), F32),
                   jax.ShapeDtypeStruct((nc, N_EXPERTS, LANES), I32)],
        compiler_params=_params("parallel"),
        name="moe_route",
    )(a, wo, h, gain, wr)


def _moe_plan(padded, n_tokens):
    nc = padded.shape[0]
    g = MOE_GRANULE
    loc_off = jnp.cumsum(padded, axis=1) - padded
    total = jnp.sum(padded, axis=0)
    total_pad = ((total + MOE_ROW_TILE - 1) // MOE_ROW_TILE) * MOE_ROW_TILE
    gbase = jnp.cumsum(total_pad) - total_pad
    coff = jnp.cumsum(padded, axis=0) - padded
    shift = gbase[None, :] + coff - loc_off
    step = shift[:, 1:] - shift[:, :-1]
    ends = (loc_off + padded)[:, :-1]
    row = jnp.arange(MOE_GRANULES, dtype=I32)[None, :, None] * g
    dest = row[:, :, 0] + shift[:, :1] + jnp.sum(jnp.where(ends[:, None, :] <= row, step[:, None, :], 0), axis=2)
    n_used = jnp.sum(padded, axis=1) // g
    valid = jnp.arange(MOE_GRANULES, dtype=I32)[None, :] < n_used[:, None]
    gdest = jnp.where(valid, dest // g, 0).astype(I32).reshape(-1)
    n_tiles = _moe_tiles(n_tokens)
    tile_row = jnp.arange(n_tiles, dtype=I32) * MOE_ROW_TILE
    tile_expert = jnp.minimum(jnp.sum(((gbase + total_pad)[None, :] <= tile_row[:, None]).astype(I32), axis=1),
                              N_EXPERTS - 1).astype(I32)
    tiles_used = ((gbase[-1] + total_pad[-1]) // MOE_ROW_TILE).astype(I32).reshape(1)
    tail_start = ((gbase + total) // g).astype(I32)
    tail_count = ((total_pad - total) // g).astype(I32)
    return gdest, n_used.astype(I32), tile_expert, tiles_used, tail_start, tail_count


def _moe_tiles(n_tokens):
    rows = (n_tokens // MOE_CHUNK) * MOE_CHUNK_ROWS + N_EXPERTS * MOE_ROW_TILE
    return -(-rows // MOE_ROW_TILE)


def _granule_copy(src, dst, sem):
    return pltpu.make_async_copy(src, dst, sem)


def _moe_scatter_kernel(gd_ref, nu_ref, ts_ref, tc_ref, nt_ref, xs_ref, out_ref, zero_ref, sem, zsem, tsem):
    g = MOE_GRANULE
    c = pl.program_id(0)
    n_tiles = out_ref.shape[0] // MOE_ROW_TILE

    def zero_copy(e, k):
        return _granule_copy(zero_ref.at[pl.ds(0, g)],
                             out_ref.at[pl.ds(pl.multiple_of((ts_ref[e] + k) * g, g), g)], zsem)

    def for_tail(action):
        for e in range(N_EXPERTS):
            def step(k, carry):
                @pl.when(k < tc_ref[e])
                def _():
                    action(zero_copy(e, k))
                return carry
            lax.fori_loop(0, MOE_ROW_TILE // g, step, 0, unroll=MOE_DMA_UNROLL)

    def for_unused_tiles(action):
        def step(i, carry):
            @pl.when(i >= nt_ref[0])
            def _():
                action(_granule_copy(
                    zero_ref, out_ref.at[pl.ds(pl.multiple_of(i * MOE_ROW_TILE, MOE_ROW_TILE), MOE_ROW_TILE)], tsem))
            return carry
        lax.fori_loop(0, n_tiles, step, 0)

    def copy(j):
        d = gd_ref[c * MOE_GRANULES + j]
        return _granule_copy(xs_ref.at[pl.ds(pl.multiple_of(j * g, g), g)],
                             out_ref.at[pl.ds(pl.multiple_of(d * g, g), g)], sem)

    def for_used(action):
        def step(j, carry):
            @pl.when(j < nu_ref[c])
            def _():
                action(copy(j))
            return carry
        lax.fori_loop(0, MOE_GRANULES, step, 0, unroll=MOE_DMA_UNROLL)

    for_used(lambda cp: cp.start())

    @pl.when(c == 0)
    def _():
        zero_ref[...] = jnp.zeros_like(zero_ref)
        for_tail(lambda cp: cp.start())
        for_unused_tiles(lambda cp: cp.start())
        for_tail(lambda cp: cp.wait())
        for_unused_tiles(lambda cp: cp.wait())

    for_used(lambda cp: cp.wait())


def _moe_scatter(gdest, n_used, tail_start, tail_count, tiles_used, xs, n_rows):
    return pl.pallas_call(
        _moe_scatter_kernel,
        grid_spec=pltpu.PrefetchScalarGridSpec(
            num_scalar_prefetch=5,
            grid=(n_used.shape[0],),
            in_specs=[pl.BlockSpec((MOE_CHUNK_ROWS, D_MODEL), lambda i, *_: (i, 0))],
            out_specs=pl.BlockSpec(memory_space=pl.ANY),
            scratch_shapes=[pltpu.VMEM((MOE_ROW_TILE, D_MODEL), BF16), pltpu.SemaphoreType.DMA(()),
                            pltpu.SemaphoreType.DMA(()), pltpu.SemaphoreType.DMA(())]),
        out_shape=jax.ShapeDtypeStruct((n_rows, D_MODEL), BF16),
        compiler_params=_params("arbitrary"),
        name="moe_scatter",
    )(gdest, n_used, tail_start, tail_count, tiles_used, xs)


def _moe_expert_kernel(te_ref, nt_ref, x_ref, wg_ref, wu_ref, wd_ref, o_ref, acc_ref):
    i = pl.program_id(0)
    f = pl.program_id(1)
    last = pl.num_programs(1) - 1
    live = i < nt_ref[0]

    @pl.when(live)
    def _():
        @pl.when(f == 0)
        def _():
            acc_ref[...] = jnp.zeros_like(acc_ref)

        x = x_ref[...]
        gate = _dot(x, wg_ref[...])
        up = _dot(x, wu_ref[...])
        act = (gate * jax.nn.sigmoid(gate) * up).astype(BF16)
        acc_ref[...] += _dot(act, wd_ref[...])

        @pl.when(f == last)
        def _():
            o_ref[...] = acc_ref[...].astype(o_ref.dtype)

    @pl.when(jnp.logical_not(live) & (f == last))
    def _():
        o_ref[...] = jnp.zeros_like(o_ref)


def _moe_expert(tile_expert, tiles_used, xg, wg, wu, wd):
    n_rows = xg.shape[0]
    nt = n_rows // MOE_ROW_TILE
    nf = EXPERT_DIM // MOE_F_TILE

    def tile(i, nt_ref):
        return jnp.minimum(i, nt_ref[0] - 1)

    def fcol(i, f, nt_ref):
        return jnp.where(i < nt_ref[0], f, nf - 1)

    return pl.pallas_call(
        _moe_expert_kernel,
        grid_spec=pltpu.PrefetchScalarGridSpec(
            num_scalar_prefetch=2,
            grid=(nt, nf),
            in_specs=[pl.BlockSpec((MOE_ROW_TILE, D_MODEL), lambda i, f, te, ntr: (tile(i, ntr), 0)),
                      pl.BlockSpec((None, D_MODEL, MOE_F_TILE),
                                   lambda i, f, te, ntr: (te[tile(i, ntr)], 0, fcol(i, f, ntr))),
                      pl.BlockSpec((None, D_MODEL, MOE_F_TILE),
                                   lambda i, f, te, ntr: (te[tile(i, ntr)], 0, fcol(i, f, ntr))),
                      pl.BlockSpec((None, MOE_F_TILE, D_MODEL),
                                   lambda i, f, te, ntr: (te[tile(i, ntr)], fcol(i, f, ntr), 0))],
            out_specs=pl.BlockSpec((MOE_ROW_TILE, D_MODEL), lambda i, f, te, ntr: (i, 0)),
            scratch_shapes=[pltpu.VMEM((MOE_ROW_TILE, D_MODEL), F32)]),
        out_shape=jax.ShapeDtypeStruct((n_rows, D_MODEL), BF16),
        compiler_params=_params("arbitrary", "arbitrary"),
        name="moe_expert",
    )(tile_expert, tiles_used, xg, wg, wu, wd)


def _moe_combine_kernel(gd_ref, nu_ref, h_ref, pos_ref, gate_ref, fg_ref, ys_ref, o_ref, ybuf_ref, sem,
                        *, final_norm):
    c = pl.program_id(0)
    nc = pl.num_programs(0)
    g = MOE_GRANULE
    t = MOE_CHUNK
    slot = c % 2

    def copy(chunk, j):
        d = gd_ref[chunk * MOE_GRANULES + j]
        return _granule_copy(ys_ref.at[pl.ds(pl.multiple_of(d * g, g), g)],
                             ybuf_ref.at[chunk % 2, pl.ds(pl.multiple_of(j * g, g), g)], sem.at[chunk % 2])

    def for_used(chunk, action):
        def step(j, carry):
            @pl.when(j < nu_ref[chunk])
            def _():
                action(copy(chunk, j))
            return carry
        lax.fori_loop(0, MOE_GRANULES, step, 0, unroll=MOE_DMA_UNROLL)

    @pl.when(c == 0)
    def _():
        ybuf_ref[...] = jnp.zeros_like(ybuf_ref)
        for_used(c, lambda cp: cp.start())

    @pl.when(c + 1 < nc)
    def _():
        for_used(c + 1, lambda cp: cp.start())

    for_used(c, lambda cp: cp.wait())

    rio = lax.broadcasted_iota(I32, (MOE_CHUNK_ROWS, t), 0)
    w = (jnp.where(rio == pos_ref[0:1, :], gate_ref[0:1, :], 0.0)
         + jnp.where(rio == pos_ref[1:2, :], gate_ref[1:2, :], 0.0)).astype(BF16)
    y = lax.dot_general(w, ybuf_ref[slot], (((0,), (0,)), ((), ())), preferred_element_type=F32)
    out = h_ref[...] + y
    if final_norm:
        out = _rms(out, fg_ref[...])
    o_ref[...] = out


def _moe_combine(gdest, n_used, h, pos, gates, ys, final_gain, final_norm):
    n = h.shape[0]
    t = MOE_CHUNK
    nc = n // t
    return pl.pallas_call(
        functools.partial(_moe_combine_kernel, final_norm=final_norm),
        grid_spec=pltpu.PrefetchScalarGridSpec(
            num_scalar_prefetch=2,
            grid=(nc,),
            in_specs=[pl.BlockSpec((t, D_MODEL), lambda i, gd, nu: (i, 0)),
                      pl.BlockSpec((None, N_EXPERTS, t), lambda i, gd, nu: (i, 0, 0)),
                      pl.BlockSpec((None, N_EXPERTS, t), lambda i, gd, nu: (i, 0, 0)),
                      pl.BlockSpec(final_gain.shape, lambda i, gd, nu: (0, 0)),
                      pl.BlockSpec(memory_space=pl.ANY)],
            out_specs=pl.BlockSpec((t, D_MODEL), lambda i, gd, nu: (i, 0)),
            scratch_shapes=[pltpu.VMEM((2, MOE_CHUNK_ROWS, D_MODEL), BF16), pltpu.SemaphoreType.DMA((2,))]),
        out_shape=jax.ShapeDtypeStruct((n, D_MODEL), F32),
        compiler_params=_params("arbitrary"),
        name="moe_combine",
    )(gdest, n_used, h, pos, gates, final_gain, ys)


def _moe(a, wo, h, gain, router, wg, wu, wd, final_gain, final_norm):
    n = h.shape[0]
    wr = jnp.pad(router, ((0, 0), (0, LANES - N_EXPERTS)))
    wr_hi = wr.astype(BF16)
    wr_lo = (wr - wr_hi.astype(F32)).astype(BF16)
    h, xs, pos, gates, cnt = _moe_route(a, wo, h, gain, jnp.concatenate([wr_hi, wr_lo], axis=1))
    gdest, n_used, tile_expert, tiles_used, tail_start, tail_count = _moe_plan(cnt[:, :, 0], n)
    xg = _moe_scatter(gdest, n_used, tail_start, tail_count, tiles_used, xs, _moe_tiles(n) * MOE_ROW_TILE)
    ys = _moe_expert(tile_expert, tiles_used, xg, wg.astype(BF16), wu.astype(BF16), wd.astype(BF16))
    return _moe_combine(gdest, n_used, h, pos, gates, ys, final_gain, final_norm)


def _rot_cols(w):
    half = MLA_ROPE // 2
    return jnp.concatenate([-w[..., half:], w[..., :half]], axis=-1)


def _rope_table(seq):
    inv = 1.0 / (ROPE_THETA ** (jnp.arange(0, MLA_ROPE, 2, dtype=F32) / MLA_ROPE))
    ang = jnp.arange(seq, dtype=F32)[:, None] * inv[None, :]
    cos, sin = jnp.cos(ang), jnp.sin(ang)
    return jnp.concatenate([cos, cos, sin, sin], axis=-1)


def _mla_weights(w_dqkv, w_uq, w_uk, w_uv):
    lat = MLA_Q_RANK + MLA_KV_RANK
    rope = w_dqkv[:, lat:]
    wd = jnp.concatenate([w_dqkv[:, :lat], rope, _rot_cols(rope)], axis=1).astype(BF16)
    q_rope = w_uq[..., MLA_NOPE:]
    wuq = jnp.concatenate([w_uq[..., :MLA_NOPE], q_rope, _rot_cols(q_rope)], axis=-1)
    wuq = wuq.reshape(MLA_Q_RANK, MLA_HEADS * MLA_HEAD_LANES).astype(BF16)
    wuk = w_uk.reshape(MLA_KV_RANK, -1).astype(BF16)
    wuvt = w_uv.reshape(MLA_KV_RANK, -1).T.astype(BF16)
    return wd, wuq, wuk, wuvt


def _swa_weights(w_qkv):
    qw = SWA_Q_HEADS * SWA_HEAD_DIM
    kvw = SWA_KV_HEADS * SWA_HEAD_DIM
    dup = lambda w: jnp.concatenate([w.reshape(D_MODEL, SWA_KV_HEADS, 1, SWA_HEAD_DIM)] * 2, axis=2).reshape(D_MODEL, -1)
    wq = w_qkv[:, :qw] * (SWA_HEAD_DIM ** -0.5 * LOG2_E)
    return jnp.concatenate([wq, dup(w_qkv[:, qw:qw + kvw]), dup(w_qkv[:, qw + kvw:])], axis=1).astype(BF16)


def kernel(x, mla_norm, mla_w_dqkv, mla_q_norm, mla_w_uq, mla_kv_norm, mla_w_uk, mla_w_uv, mla_w_o, swa_norm, swa_w_qkv, swa_sink, swa_w_o, ffn_norm, ffn_w_gate, ffn_w_up, ffn_w_down, moe_norm, moe_router, moe_w_gate, moe_w_up, moe_w_down, final_norm):
    batch, seq, _ = x.shape
    h = x.reshape(batch * seq, D_MODEL)
    cs = _rope_table(seq)
    row = lambda v: v.reshape(1, -1).astype(F32)
    depth = 2 * mla_norm.shape[0]
    for layer in range(depth):
        j = layer // 2
        if layer % 2 == 0:
            wd, wuq, wuk, wuvt = _mla_weights(mla_w_dqkv[j], mla_w_uq[j], mla_w_uk[j], mla_w_uv[j])
            q, k, vt = _mla_proj(h, batch, seq, row(mla_norm[j]), wd, row(mla_q_norm[j]), row(mla_kv_norm[j]),
                                 wuq, wuk, wuvt, cs)
            a = _mla_attn(q, k, vt, batch, seq)
            h = _ffn(a, mla_w_o[j].astype(BF16), h, row(ffn_norm[j]), ffn_w_gate[j].astype(BF16),
                     ffn_w_up[j].astype(BF16), ffn_w_down[j].astype(BF16))
        else:
            qkv = _norm_matmul(h, row(swa_norm[j]), _swa_weights(swa_w_qkv[j]))
            a = _swa_attn(qkv, swa_sink[j].astype(F32), batch, seq)
            h = _moe(a, swa_w_o[j].astype(BF16), h, row(moe_norm[j]), moe_router[j], moe_w_gate[j], moe_w_up[j],
                     moe_w_down[j], row(final_norm), layer == depth - 1)
    return h.reshape(batch, seq, D_MODEL)
```

```python
import functools

import jax
import jax.numpy as jnp
from jax import lax
from jax.experimental import pallas as pl
from jax.experimental.pallas import tpu as pltpu

F32 = jnp.float32
BF16 = jnp.bfloat16
I32 = jnp.int32

D_MODEL = 1024
EPS = 1e-6
NEG_INF = -1e30
ROPE_THETA = 10000.0
LOG2_E = 1.4426950408889634

MLA_HEADS = 8
MLA_Q_RANK = 256
MLA_KV_RANK = 256
MLA_NOPE = 128
MLA_ROPE = 64
MLA_V = 128
MLA_QK = MLA_NOPE + MLA_ROPE
MLA_HEAD_LANES = 256

SWA_Q_HEADS = 16
SWA_KV_HEADS = 4
SWA_GROUP = 4
SWA_HEAD_DIM = 64
SWA_WINDOW = 128
SWA_BLOCK = 128
SWA_LOOKAHEAD = 2
SWA_STEP_BLOCKS = 4

FFN_DIM = 2816
N_EXPERTS = 8
EXPERT_DIM = 2048

LANES = 128
VMEM_LIMIT = 48 * 1024 * 1024

ROW_TILE = 512
ATTN_Q_TILE = 512
ATTN_STEP_HEADS = 2
ATTN_LOOKAHEAD = 2
MOE_CHUNK = 512
MOE_GRANULE = 16
MOE_CHUNK_ROWS = 2 * MOE_CHUNK + N_EXPERTS * MOE_GRANULE
MOE_GRANULES = MOE_CHUNK_ROWS // MOE_GRANULE
MOE_DMA_UNROLL = 8
MOE_ROW_TILE = 512
MOE_F_TILE = 2048


def _params(*sem, flags=None):
    return pltpu.CompilerParams(dimension_semantics=sem, vmem_limit_bytes=VMEM_LIMIT, flags=flags)


def _rms(x, g):
    return x * lax.rsqrt(jnp.mean(x * x, axis=-1, keepdims=True) + EPS) * g


def _dot(a, b):
    return jnp.dot(a, b, preferred_element_type=F32)


def _dot_nt(a, b):
    return lax.dot_general(a, b, (((1,), (1,)), ((), ())), preferred_element_type=F32)


def _mla_proj_kernel(h_ref, g_ref, wd_ref, qn_ref, kvn_ref, wuq_ref, wuk_ref, wuvt_ref, cs_ref,
                     q_ref, k_ref, vt_ref):
    xn = _rms(h_ref[...], g_ref[...]).astype(BF16)
    down = _dot(xn, wd_ref[...])
    cq = _rms(down[:, :MLA_Q_RANK], qn_ref[...]).astype(BF16)
    ckv = _rms(down[:, MLA_Q_RANK:MLA_Q_RANK + MLA_KV_RANK], kvn_ref[...]).astype(BF16)
    cs = cs_ref[...]
    lane = lax.broadcasted_iota(I32, cs.shape, 1)

    def rope(a):
        p = a * cs
        return p + pltpu.roll(p, MLA_ROPE, 1)

    kr = jnp.where(lane < MLA_ROPE, rope(down[:, 2 * MLA_Q_RANK:]), 0.0)
    q = _dot(cq, wuq_ref[...]) * (MLA_QK ** -0.5 * LOG2_E)
    for h in range(MLA_HEADS):
        c = h * MLA_HEAD_LANES
        q_ref[:, c:c + LANES] = q[:, c:c + LANES].astype(BF16)
        q_ref[:, c + LANES:c + 2 * LANES] = rope(q[:, c + LANES:c + 2 * LANES]).astype(BF16)
    nk = MLA_HEADS * MLA_NOPE
    k_ref[:, :nk] = _dot(ckv, wuk_ref[...]).astype(BF16)
    k_ref[:, nk:] = kr.astype(BF16)
    vt_ref[...] = _dot_nt(wuvt_ref[...], ckv).astype(BF16)


def _mla_proj(h, batch, seq, gain, wd, qn, kvn, wuq, wuk, wuvt, cs):
    n = h.shape[0]
    t = ROW_TILE
    sb = seq // t
    full = lambda a: pl.BlockSpec(a.shape, lambda i: (0, 0))
    return pl.pallas_call(
        _mla_proj_kernel,
        grid=(n // t,),
        in_specs=[pl.BlockSpec((t, D_MODEL), lambda i: (i, 0)), full(gain), full(wd), full(qn), full(kvn),
                  full(wuq), full(wuk), full(wuvt), pl.BlockSpec((t, LANES), lambda i: (i % sb, 0))],
        out_specs=[pl.BlockSpec((t, MLA_HEADS * MLA_HEAD_LANES), lambda i: (i, 0)),
                   pl.BlockSpec((t, MLA_HEADS * MLA_NOPE + LANES), lambda i: (i, 0)),
                   pl.BlockSpec((None, MLA_HEADS * MLA_V, t), lambda i: (i // sb, 0, i % sb))],
        out_shape=[jax.ShapeDtypeStruct((n, MLA_HEADS * MLA_HEAD_LANES), BF16),
                   jax.ShapeDtypeStruct((n, MLA_HEADS * MLA_NOPE + LANES), BF16),
                   jax.ShapeDtypeStruct((batch, MLA_HEADS * MLA_V, seq), BF16)],
        compiler_params=_params("parallel"),
        name="mla_proj",
    )(h, gain, wd, qn, kvn, wuq, wuk, wuvt, cs)


def _mla_attn_kernel(q_ref, kn_ref, kr_ref, vt_ref, o_ref, kcat_ref):
    hp = ATTN_STEP_HEADS
    for h in range(hp):
        kcat_ref[h, :, :LANES] = kn_ref[:, h * LANES:(h + 1) * LANES]
        kcat_ref[h, :, LANES:] = kr_ref[...]
    seq = q_ref.shape[0]
    tq = ATTN_Q_TILE
    work = [(h, t) for h in range(hp) for t in range(seq // tq)]

    def scores(h, t):
        return _dot_nt(kcat_ref[h], q_ref[t * tq:(t + 1) * tq, h * MLA_HEAD_LANES:(h + 1) * MLA_HEAD_LANES])

    def attend(h, st):
        pt = jnp.exp2(st - jnp.max(st, axis=0, keepdims=True))
        l = jnp.sum(pt, axis=0, keepdims=True)
        ot = _dot(vt_ref[h * MLA_V:(h + 1) * MLA_V, :], pt.astype(BF16)) / l
        return ot.T.astype(o_ref.dtype)

    ahead = ATTN_LOOKAHEAD
    pending = [scores(*w) for w in work[:ahead]]
    outs = {}
    for n, (h, t) in enumerate(work):
        if n + ahead < len(work):
            pending.append(scores(*work[n + ahead]))
        outs[h, t] = attend(h, pending.pop(0))
    o_ref[...] = jnp.concatenate(
        [jnp.concatenate([outs[h, t] for t in range(seq // tq)], axis=0) for h in range(hp)], axis=1)


def _mla_attn(q, k, vt, batch, seq):
    n = q.shape[0]
    hp = ATTN_STEP_HEADS
    return pl.pallas_call(
        _mla_attn_kernel,
        grid=(batch, MLA_HEADS // hp),
        in_specs=[pl.BlockSpec((seq, hp * MLA_HEAD_LANES), lambda b, h: (b, h)),
                  pl.BlockSpec((seq, hp * LANES), lambda b, h: (b, h)),
                  pl.BlockSpec((seq, LANES), lambda b, h: (b, MLA_HEADS)),
                  pl.BlockSpec((None, hp * MLA_V, seq), lambda b, h: (b, h, 0))],
        out_specs=pl.BlockSpec((seq, hp * MLA_V), lambda b, h: (b, h)),
        out_shape=jax.ShapeDtypeStruct((n, MLA_HEADS * MLA_V), BF16),
        scratch_shapes=[pltpu.VMEM((hp, seq, MLA_HEAD_LANES), BF16)],
        compiler_params=_params("parallel", "arbitrary"),
        name="mla_attn",
    )(q, k, k, vt)


def _ffn_kernel(a_ref, wo_ref, h_ref, g_ref, wg_ref, wu_ref, wd_ref, o_ref):
    h1 = h_ref[...] + _dot(a_ref[...], wo_ref[...])
    xn = _rms(h1, g_ref[...]).astype(BF16)
    gate = _dot(xn, wg_ref[...])
    up = _dot(xn, wu_ref[...])
    act = (gate * jax.nn.sigmoid(gate) * up).astype(BF16)
    o_ref[...] = h1 + _dot(act, wd_ref[...])


def _resident(a):
    return pl.BlockSpec(a.shape, lambda i: (0,) * a.ndim, pipeline_mode=pl.Buffered(1))


def _ffn(a, wo, h, gain, wg, wu, wd):
    n = h.shape[0]
    t = ROW_TILE
    return pl.pallas_call(
        _ffn_kernel,
        grid=(n // t,),
        in_specs=[pl.BlockSpec((t, a.shape[1]), lambda i: (i, 0)), _resident(wo),
                  pl.BlockSpec((t, D_MODEL), lambda i: (i, 0)), _resident(gain),
                  _resident(wg), _resident(wu), _resident(wd)],
        out_specs=pl.BlockSpec((t, D_MODEL), lambda i: (i, 0)),
        out_shape=jax.ShapeDtypeStruct((n, D_MODEL), F32),
        compiler_params=_params("parallel"),
        name="ffn",
    )(a, wo, h, gain, wg, wu, wd)


def _norm_matmul_kernel(h_ref, g_ref, w_ref, o_ref):
    o_ref[...] = _dot(_rms(h_ref[...], g_ref[...]).astype(BF16), w_ref[...]).astype(o_ref.dtype)


def _norm_matmul(h, gain, w):
    n = h.shape[0]
    t = ROW_TILE
    return pl.pallas_call(
        _norm_matmul_kernel,
        grid=(n // t,),
        in_specs=[pl.BlockSpec((t, D_MODEL), lambda i: (i, 0)),
                  pl.BlockSpec(gain.shape, lambda i: (0, 0)),
                  pl.BlockSpec(w.shape, lambda i: (0, 0))],
        out_specs=pl.BlockSpec((t, w.shape[1]), lambda i: (i, 0)),
        out_shape=jax.ShapeDtypeStruct((n, w.shape[1]), BF16),
        compiler_params=_params("parallel"),
        name="norm_matmul",
    )(h, gain, w)


def _alibi_slope(head):
    return 2.0 ** (-8.0 * (head + 1) / SWA_Q_HEADS)


def _swa_attn_kernel(sink_ref, q_ref, k_ref, v_ref, o_ref):
    blk = SWA_BLOCK
    nb = k_ref.shape[0] // blk
    pair_shape = (blk, 2 * blk)
    row_iota = lax.broadcasted_iota(I32, pair_shape, 0)
    col = lax.broadcasted_iota(I32, pair_shape, 1)
    kcol = jnp.where(col < blk, col, col - blk)
    lane = lax.broadcasted_iota(I32, (blk, LANES), 1)
    lo = lane < SWA_HEAD_DIM

    def block(jb, carry):
        i = pl.program_id(1) * SWA_STEP_BLOCKS + jb
        q0 = pl.multiple_of(jb * blk, blk)
        qpos = i * blk + row_iota

        neg_dist, rows = [], []
        for j in (-1, 0, 1):
            kb = i + j
            kpos = kb * blk + kcol
            dist = jnp.abs(qpos - kpos)
            valid = (dist <= SWA_WINDOW) & (kb >= 0) & (kb < nb)
            neg_dist.append(jnp.where(valid, -dist.astype(F32), NEG_INF))
            rows.append(pl.multiple_of(jnp.clip(kb, 0, nb - 1) * blk, blk))

        def split(ref, g, r0):
            d = ref[pl.ds(r0, blk), g * LANES:(g + 1) * LANES]
            return jnp.concatenate([jnp.where(lo, d, 0), jnp.where(lo, 0, d)], axis=0)

        def scores(pair):
            h0 = 2 * pair
            qp = q_ref[pl.ds(q0, blk), pair * LANES:(pair + 1) * LANES]
            slope = jnp.where(col[0:1, :] < blk, _alibi_slope(h0) * LOG2_E, _alibi_slope(h0 + 1) * LOG2_E)
            return [_dot_nt(qp, split(k_ref, h0 // SWA_GROUP, rows[j])) + slope * neg_dist[j] for j in range(3)]

        def attend(pair, ss):
            h0 = 2 * pair
            ps, ls = [], []
            for half, head in ((0, h0), (1, h0 + 1)):
                sl = slice(half * blk, (half + 1) * blk)
                snk = sink_ref[head] * LOG2_E
                m = jnp.max(jnp.maximum(jnp.maximum(ss[0][:, sl], ss[1][:, sl]), ss[2][:, sl]),
                            axis=-1, keepdims=True)
                m = jnp.maximum(m, snk)
                p3 = [jnp.exp2(ss[j][:, sl] - m) for j in range(3)]
                ls.append(jnp.sum(p3[0] + p3[1] + p3[2], axis=-1, keepdims=True) + jnp.exp2(snk - m))
                ps.append(p3)
            o = jnp.zeros((blk, LANES), F32)
            for j in range(3):
                pj = jnp.concatenate([ps[0][j], ps[1][j]], axis=1).astype(BF16)
                o = o + _dot(pj, split(v_ref, h0 // SWA_GROUP, rows[j]))
            return (o / jnp.where(lo, ls[0], ls[1])).astype(o_ref.dtype)

        n_pairs = SWA_Q_HEADS // 2
        pending = {p: scores(p) for p in range(min(SWA_LOOKAHEAD, n_pairs))}
        outs = []
        for p in range(n_pairs):
            if p + SWA_LOOKAHEAD < n_pairs:
                pending[p + SWA_LOOKAHEAD] = scores(p + SWA_LOOKAHEAD)
            outs.append(attend(p, pending.pop(p)))
        o_ref[pl.ds(q0, blk), :] = jnp.concatenate(outs, axis=1)
        return carry

    lax.fori_loop(0, SWA_STEP_BLOCKS, block, 0)


def _swa_attn(qkv, sink, batch, seq):
    n = qkv.shape[0]
    rows = SWA_STEP_BLOCKS * SWA_BLOCK
    steps = seq // rows
    qw = SWA_Q_HEADS * SWA_HEAD_DIM
    kvw = SWA_KV_HEADS * LANES
    kblk = qw // kvw
    return pl.pallas_call(
        _swa_attn_kernel,
        grid=(batch, steps),
        in_specs=[pl.BlockSpec(memory_space=pltpu.SMEM),
                  pl.BlockSpec((rows, qw), lambda b, i: (b * steps + i, 0)),
                  pl.BlockSpec((seq, kvw), lambda b, i: (b, kblk)),
                  pl.BlockSpec((seq, kvw), lambda b, i: (b, kblk + 1))],
        out_specs=pl.BlockSpec((rows, qw), lambda b, i: (b * steps + i, 0)),
        out_shape=jax.ShapeDtypeStruct((n, qw), BF16),
        compiler_params=_params("parallel", "arbitrary"),
        name="swa_attn",
    )(sink, qkv, qkv, qkv)


def _moe_route_kernel(a_ref, wo_ref, h_ref, g_ref, wr_ref, h1_ref, xs_ref, pos_ref, gate_ref, cnt_ref):
    t = MOE_CHUNK
    h1 = h_ref[...] + _dot(a_ref[...], wo_ref[...])
    h1_ref[...] = h1
    xn = _rms(h1, g_ref[...])
    x_hi = xn.astype(BF16)
    x_lo = (xn - x_hi.astype(F32)).astype(BF16)
    hi = _dot(x_hi, wr_ref[...])
    logits = hi[:, :LANES] + hi[:, LANES:] + _dot(x_lo, wr_ref[:, :LANES])
    lt = logits.T[:N_EXPERTS]
    eio = lax.broadcasted_iota(I32, lt.shape, 0)
    m1 = jnp.max(lt, axis=0, keepdims=True)
    i1 = jnp.min(jnp.where(lt == m1, eio, N_EXPERTS), axis=0, keepdims=True)
    l2 = jnp.where(eio == i1, -jnp.inf, lt)
    m2 = jnp.max(l2, axis=0, keepdims=True)
    i2 = jnp.min(jnp.where(l2 == m2, eio, N_EXPERTS), axis=0, keepdims=True)
    e21 = jnp.exp(m2 - m1)
    g1 = 1.0 / (1.0 + e21)
    g2 = e21 * g1

    sel1 = eio == i1
    sel2 = eio == i2
    oh = jnp.where(sel1 | sel2, 1.0, 0.0)
    tr = lax.broadcasted_iota(I32, (t, t), 0)
    tc = lax.broadcasted_iota(I32, (t, t), 1)
    before = jnp.where(tr < tc, 1.0, 0.0).astype(BF16)
    rank = _dot(oh.astype(BF16), before)
    cnt = jnp.sum(oh, axis=1, keepdims=True).astype(I32)
    padded = jnp.bitwise_and(cnt + (MOE_GRANULE - 1), -MOE_GRANULE)
    offs, run = [], jnp.zeros((1, 1), I32)
    for e in range(N_EXPERTS):
        offs.append(run)
        run = run + padded[e:e + 1, :]
    off = jnp.concatenate(offs, axis=0).astype(F32)
    slot = rank + off
    pos1 = jnp.sum(jnp.where(sel1, slot, 0.0), axis=0, keepdims=True).astype(I32)
    pos2 = jnp.sum(jnp.where(sel2, slot, 0.0), axis=0, keepdims=True).astype(I32)

    rio = lax.broadcasted_iota(I32, (MOE_CHUNK_ROWS, t), 0)
    perm = jnp.where(rio == pos1, 1.0, jnp.where(rio == pos2, 1.0, 0.0)).astype(BF16)
    xs_ref[...] = _dot(perm, x_hi).astype(BF16)

    zi = jnp.zeros((N_EXPERTS - 2, t), I32)
    pos_ref[...] = jnp.concatenate([pos1, pos2, zi], axis=0)
    gate_ref[...] = jnp.concatenate([g1, g2, zi.astype(F32)], axis=0)
    cnt_ref[...] = jnp.broadcast_to(padded, (N_EXPERTS, LANES))


def _moe_route(a, wo, h, gain, wr):
    n = h.shape[0]
    t = MOE_CHUNK
    nc = n // t
    return pl.pallas_call(
        _moe_route_kernel,
        grid=(nc,),
        in_specs=[pl.BlockSpec((t, a.shape[1]), lambda i: (i, 0)),
                  pl.BlockSpec(wo.shape, lambda i: (0, 0)),
                  pl.BlockSpec((t, D_MODEL), lambda i: (i, 0)),
                  pl.BlockSpec(gain.shape, lambda i: (0, 0)),
                  pl.BlockSpec(wr.shape, lambda i: (0, 0))],
        out_specs=[pl.BlockSpec((t, D_MODEL), lambda i: (i, 0)),
                   pl.BlockSpec((MOE_CHUNK_ROWS, D_MODEL), lambda i: (i, 0)),
                   pl.BlockSpec((None, N_EXPERTS, t), lambda i: (i, 0, 0)),
                   pl.BlockSpec((None, N_EXPERTS, t), lambda i: (i, 0, 0)),
                   pl.BlockSpec((None, N_EXPERTS, LANES), lambda i: (i, 0, 0))],
        out_shape=[jax.ShapeDtypeStruct((n, D_MODEL), F32),
                   jax.ShapeDtypeStruct((nc * MOE_CHUNK_ROWS, D_MODEL), BF16),
                   jax.ShapeDtypeStruct((nc, N_EXPERTS, t), I32),
                   jax.ShapeDtypeStruct((nc, N_EXPERTS, t), F32),
                   jax.ShapeDtypeStruct((nc, N_EXPERTS, LANES), I32)],
        compiler_params=_params("parallel"),
        name="moe_route",
    )(a, wo, h, gain, wr)


def _moe_plan(padded, n_tokens):
    nc = padded.shape[0]
    g = MOE_GRANULE
    loc_off = jnp.cumsum(padded, axis=1) - padded
    total = jnp.sum(padded, axis=0)
    total_pad = ((total + MOE_ROW_TILE - 1) // MOE_ROW_TILE) * MOE_ROW_TILE
    gbase = jnp.cumsum(total_pad) - total_pad
    coff = jnp.cumsum(padded, axis=0) - padded
    shift = gbase[None, :] + coff - loc_off
    step = shift[:, 1:] - shift[:, :-1]
    ends = (loc_off + padded)[:, :-1]
    row = jnp.arange(MOE_GRANULES, dtype=I32)[None, :, None] * g
    dest = row[:, :, 0] + shift[:, :1] + jnp.sum(jnp.where(ends[:, None, :] <= row, step[:, None, :], 0), axis=2)
    n_used = jnp.sum(padded, axis=1) // g
    valid = jnp.arange(MOE_GRANULES, dtype=I32)[None, :] < n_used[:, None]
    gdest = jnp.where(valid, dest // g, 0).astype(I32).reshape(-1)
    n_tiles = _moe_tiles(n_tokens)
    tile_row = jnp.arange(n_tiles, dtype=I32) * MOE_ROW_TILE
    tile_expert = jnp.minimum(jnp.sum(((gbase + total_pad)[None, :] <= tile_row[:, None]).astype(I32), axis=1),
                              N_EXPERTS - 1).astype(I32)
    tiles_used = ((gbase[-1] + total_pad[-1]) // MOE_ROW_TILE).astype(I32).reshape(1)
    tail_start = ((gbase + total) // g).astype(I32)
    tail_count = ((total_pad - total) // g).astype(I32)
    return gdest, n_used.astype(I32), tile_expert, tiles_used, tail_start, tail_count


def _moe_tiles(n_tokens):
    rows = (n_tokens // MOE_CHUNK) * MOE_CHUNK_ROWS + N_EXPERTS * MOE_ROW_TILE
    return -(-rows // MOE_ROW_TILE)


def _granule_copy(src, dst, sem):
    return pltpu.make_async_copy(src, dst, sem)


def _moe_scatter_kernel(gd_ref, nu_ref, ts_ref, tc_ref, nt_ref, xs_ref, out_ref, zero_ref, sem, zsem, tsem):
    g = MOE_GRANULE
    c = pl.program_id(0)
    n_tiles = out_ref.shape[0] // MOE_ROW_TILE

    def zero_copy(e, k):
        return _granule_copy(zero_ref.at[pl.ds(0, g)],
                             out_ref.at[pl.ds(pl.multiple_of((ts_ref[e] + k) * g, g), g)], zsem)

    def for_tail(action):
        for e in range(N_EXPERTS):
            def step(k, carry):
                @pl.when(k < tc_ref[e])
                def _():
                    action(zero_copy(e, k))
                return carry
            lax.fori_loop(0, MOE_ROW_TILE // g, step, 0, unroll=MOE_DMA_UNROLL)

    def for_unused_tiles(action):
        def step(i, carry):
            @pl.when(i >= nt_ref[0])
            def _():
                action(_granule_copy(
                    zero_ref, out_ref.at[pl.ds(pl.multiple_of(i * MOE_ROW_TILE, MOE_ROW_TILE), MOE_ROW_TILE)], tsem))
            return carry
        lax.fori_loop(0, n_tiles, step, 0)

    def copy(j):
        d = gd_ref[c * MOE_GRANULES + j]
        return _granule_copy(xs_ref.at[pl.ds(pl.multiple_of(j * g, g), g)],
                             out_ref.at[pl.ds(pl.multiple_of(d * g, g), g)], sem)

    def for_used(action):
        def step(j, carry):
            @pl.when(j < nu_ref[c])
            def _():
                action(copy(j))
            return carry
        lax.fori_loop(0, MOE_GRANULES, step, 0, unroll=MOE_DMA_UNROLL)

    for_used(lambda cp: cp.start())

    @pl.when(c == 0)
    def _():
        zero_ref[...] = jnp.zeros_like(zero_ref)
        for_tail(lambda cp: cp.start())
        for_unused_tiles(lambda cp: cp.start())
        for_tail(lambda cp: cp.wait())
        for_unused_tiles(lambda cp: cp.wait())

    for_used(lambda cp: cp.wait())


def _moe_scatter(gdest, n_used, tail_start, tail_count, tiles_used, xs, n_rows):
    return pl.pallas_call(
        _moe_scatter_kernel,
        grid_spec=pltpu.PrefetchScalarGridSpec(
            num_scalar_prefetch=5,
            grid=(n_used.shape[0],),
            in_specs=[pl.BlockSpec((MOE_CHUNK_ROWS, D_MODEL), lambda i, *_: (i, 0))],
            out_specs=pl.BlockSpec(memory_space=pl.ANY),
            scratch_shapes=[pltpu.VMEM((MOE_ROW_TILE, D_MODEL), BF16), pltpu.SemaphoreType.DMA(()),
                            pltpu.SemaphoreType.DMA(()), pltpu.SemaphoreType.DMA(())]),
        out_shape=jax.ShapeDtypeStruct((n_rows, D_MODEL), BF16),
        compiler_params=_params("arbitrary"),
        name="moe_scatter",
    )(gdest, n_used, tail_start, tail_count, tiles_used, xs)


def _moe_expert_kernel(te_ref, nt_ref, x_ref, wg_ref, wu_ref, wd_ref, o_ref, acc_ref):
    i = pl.program_id(0)
    f = pl.program_id(1)
    last = pl.num_programs(1) - 1
    live = i < nt_ref[0]

    @pl.when(live)
    def _():
        @pl.when(f == 0)
        def _():
            acc_ref[...] = jnp.zeros_like(acc_ref)

        x = x_ref[...]
        gate = _dot(x, wg_ref[...])
        up = _dot(x, wu_ref[...])
        act = (gate * jax.nn.sigmoid(gate) * up).astype(BF16)
        acc_ref[...] += _dot(act, wd_ref[...])

        @pl.when(f == last)
        def _():
            o_ref[...] = acc_ref[...].astype(o_ref.dtype)

    @pl.when(jnp.logical_not(live) & (f == last))
    def _():
        o_ref[...] = jnp.zeros_like(o_ref)


def _moe_expert(tile_expert, tiles_used, xg, wg, wu, wd):
    n_rows = xg.shape[0]
    nt = n_rows // MOE_ROW_TILE
    nf = EXPERT_DIM // MOE_F_TILE

    def tile(i, nt_ref):
        return jnp.minimum(i, nt_ref[0] - 1)

    def fcol(i, f, nt_ref):
        return jnp.where(i < nt_ref[0], f, nf - 1)

    return pl.pallas_call(
        _moe_expert_kernel,
        grid_spec=pltpu.PrefetchScalarGridSpec(
            num_scalar_prefetch=2,
            grid=(nt, nf),
            in_specs=[pl.BlockSpec((MOE_ROW_TILE, D_MODEL), lambda i, f, te, ntr: (tile(i, ntr), 0)),
                      pl.BlockSpec((None, D_MODEL, MOE_F_TILE),
                                   lambda i, f, te, ntr: (te[tile(i, ntr)], 0, fcol(i, f, ntr))),
                      pl.BlockSpec((None, D_MODEL, MOE_F_TILE),
                                   lambda i, f, te, ntr: (te[tile(i, ntr)], 0, fcol(i, f, ntr))),
                      pl.BlockSpec((None, MOE_F_TILE, D_MODEL),
                                   lambda i, f, te, ntr: (te[tile(i, ntr)], fcol(i, f, ntr), 0))],
            out_specs=pl.BlockSpec((MOE_ROW_TILE, D_MODEL), lambda i, f, te, ntr: (i, 0)),
            scratch_shapes=[pltpu.VMEM((MOE_ROW_TILE, D_MODEL), F32)]),
        out_shape=jax.ShapeDtypeStruct((n_rows, D_MODEL), BF16),
        compiler_params=_params("arbitrary", "arbitrary"),
        name="moe_expert",
    )(tile_expert, tiles_used, xg, wg, wu, wd)


def _moe_combine_kernel(gd_ref, nu_ref, h_ref, pos_ref, gate_ref, fg_ref, ys_ref, o_ref, ybuf_ref, sem,
                        *, final_norm):
    c = pl.program_id(0)
    nc = pl.num_programs(0)
    g = MOE_GRANULE
    t = MOE_CHUNK
    slot = c % 2

    def copy(chunk, j):
        d = gd_ref[chunk * MOE_GRANULES + j]
        return _granule_copy(ys_ref.at[pl.ds(pl.multiple_of(d * g, g), g)],
                             ybuf_ref.at[chunk % 2, pl.ds(pl.multiple_of(j * g, g), g)], sem.at[chunk % 2])

    def for_used(chunk, action):
        def step(j, carry):
            @pl.when(j < nu_ref[chunk])
            def _():
                action(copy(chunk, j))
            return carry
        lax.fori_loop(0, MOE_GRANULES, step, 0, unroll=MOE_DMA_UNROLL)

    @pl.when(c == 0)
    def _():
        ybuf_ref[...] = jnp.zeros_like(ybuf_ref)
        for_used(c, lambda cp: cp.start())

    @pl.when(c + 1 < nc)
    def _():
        for_used(c + 1, lambda cp: cp.start())

    for_used(c, lambda cp: cp.wait())

    rio = lax.broadcasted_iota(I32, (MOE_CHUNK_ROWS, t), 0)
    w = (jnp.where(rio == pos_ref[0:1, :], gate_ref[0:1, :], 0.0)
         + jnp.where(rio == pos_ref[1:2, :], gate_ref[1:2, :], 0.0)).astype(BF16)
    y = lax.dot_general(w, ybuf_ref[slot], (((0,), (0,)), ((), ())), preferred_element_type=F32)
    out = h_ref[...] + y
    if final_norm:
        out = _rms(out, fg_ref[...])
    o_ref[...] = out


def _moe_combine(gdest, n_used, h, pos, gates, ys, final_gain, final_norm):
    n = h.shape[0]
    t = MOE_CHUNK
    nc = n // t
    return pl.pallas_call(
        functools.partial(_moe_combine_kernel, final_norm=final_norm),
        grid_spec=pltpu.PrefetchScalarGridSpec(
            num_scalar_prefetch=2,
            grid=(nc,),
            in_specs=[pl.BlockSpec((t, D_MODEL), lambda i, gd, nu: (i, 0)),
                      pl.BlockSpec((None, N_EXPERTS, t), lambda i, gd, nu: (i, 0, 0)),
                      pl.BlockSpec((None, N_EXPERTS, t), lambda i, gd, nu: (i, 0, 0)),
                      pl.BlockSpec(final_gain.shape, lambda i, gd, nu: (0, 0)),
                      pl.BlockSpec(memory_space=pl.ANY)],
            out_specs=pl.BlockSpec((t, D_MODEL), lambda i, gd, nu: (i, 0)),
            scratch_shapes=[pltpu.VMEM((2, MOE_CHUNK_ROWS, D_MODEL), BF16), pltpu.SemaphoreType.DMA((2,))]),
        out_shape=jax.ShapeDtypeStruct((n, D_MODEL), F32),
        compiler_params=_params("arbitrary"),
        name="moe_combine",
    )(gdest, n_used, h, pos, gates, final_gain, ys)


def _moe(a, wo, h, gain, router, wg, wu, wd, final_gain, final_norm):
    n = h.shape[0]
    wr = jnp.pad(router, ((0, 0), (0, LANES - N_EXPERTS)))
    wr_hi = wr.astype(BF16)
    wr_lo = (wr - wr_hi.astype(F32)).astype(BF16)
    h, xs, pos, gates, cnt = _moe_route(a, wo, h, gain, jnp.concatenate([wr_hi, wr_lo], axis=1))
    gdest, n_used, tile_expert, tiles_used, tail_start, tail_count = _moe_plan(cnt[:, :, 0], n)
    xg = _moe_scatter(gdest, n_used, tail_start, tail_count, tiles_used, xs, _moe_tiles(n) * MOE_ROW_TILE)
    ys = _moe_expert(tile_expert, tiles_used, xg, wg.astype(BF16), wu.astype(BF16), wd.astype(BF16))
    return _moe_combine(gdest, n_used, h, pos, gates, ys, final_gain, final_norm)


def _rot_cols(w):
    half = MLA_ROPE // 2
    return jnp.concatenate([-w[..., half:], w[..., :half]], axis=-1)


def _rope_table(seq):
    inv = 1.0 / (ROPE_THETA ** (jnp.arange(0, MLA_ROPE, 2, dtype=F32) / MLA_ROPE))
    ang = jnp.arange(seq, dtype=F32)[:, None] * inv[None, :]
    cos, sin = jnp.cos(ang), jnp.sin(ang)
    return jnp.concatenate([cos, cos, sin, sin], axis=-1)


def _mla_weights(w_dqkv, w_uq, w_uk, w_uv):
    lat = MLA_Q_RANK + MLA_KV_RANK
    rope = w_dqkv[:, lat:]
    wd = jnp.concatenate([w_dqkv[:, :lat], rope, _rot_cols(rope)], axis=1).astype(BF16)
    q_rope = w_uq[..., MLA_NOPE:]
    wuq = jnp.concatenate([w_uq[..., :MLA_NOPE], q_rope, _rot_cols(q_rope)], axis=-1)
    wuq = wuq.reshape(MLA_Q_RANK, MLA_HEADS * MLA_HEAD_LANES).astype(BF16)
    wuk = w_uk.reshape(MLA_KV_RANK, -1).astype(BF16)
    wuvt = w_uv.reshape(MLA_KV_RANK, -1).T.astype(BF16)
    return wd, wuq, wuk, wuvt


def _swa_weights(w_qkv):
    qw = SWA_Q_HEADS * SWA_HEAD_DIM
    kvw = SWA_KV_HEADS * SWA_HEAD_DIM
    dup = lambda w: jnp.concatenate([w.reshape(D_MODEL, SWA_KV_HEADS, 1, SWA_HEAD_DIM)] * 2, axis=2).reshape(D_MODEL, -1)
    wq = w_qkv[:, :qw] * (SWA_HEAD_DIM ** -0.5 * LOG2_E)
    return jnp.concatenate([wq, dup(w_qkv[:, qw:qw + kvw]), dup(w_qkv[:, qw + kvw:])], axis=1).astype(BF16)


def kernel(x, mla_norm, mla_w_dqkv, mla_q_norm, mla_w_uq, mla_kv_norm, mla_w_uk, mla_w_uv, mla_w_o, swa_norm, swa_w_qkv, swa_sink, swa_w_o, ffn_norm, ffn_w_gate, ffn_w_up, ffn_w_down, moe_norm, moe_router, moe_w_gate, moe_w_up, moe_w_down, final_norm):
    batch, seq, _ = x.shape
    h = x.reshape(batch * seq, D_MODEL)
    cs = _rope_table(seq)
    row = lambda v: v.reshape(1, -1).astype(F32)
    depth = 2 * mla_norm.shape[0]
    for layer in range(depth):
        j = layer // 2
        if layer % 2 == 0:
            wd, wuq, wuk, wuvt = _mla_weights(mla_w_dqkv[j], mla_w_uq[j], mla_w_uk[j], mla_w_uv[j])
            q, k, vt = _mla_proj(h, batch, seq, row(mla_norm[j]), wd, row(mla_q_norm[j]), row(mla_kv_norm[j]),
                                 wuq, wuk, wuvt, cs)
            a = _mla_attn(q, k, vt, batch, seq)
            h = _ffn(a, mla_w_o[j].astype(BF16), h, row(ffn_norm[j]), ffn_w_gate[j].astype(BF16),
                     ffn_w_up[j].astype(BF16), ffn_w_down[j].astype(BF16))
        else:
            qkv = _norm_matmul(h, row(swa_norm[j]), _swa_weights(swa_w_qkv[j]))
            a = _swa_attn(qkv, swa_sink[j].astype(F32), batch, seq)
            h = _moe(a, swa_w_o[j].astype(BF16), h, row(moe_norm[j]), moe_router[j], moe_w_gate[j], moe_w_up[j],
                     moe_w_down[j], row(final_norm), layer == depth - 1)
    return h.reshape(batch, seq, D_MODEL)
```

```python
import functools

import jax
import jax.numpy as jnp
from jax import lax
from jax.experimental import pallas as pl
from jax.experimental.pallas import tpu as pltpu

F32 = jnp.float32
BF16 = jnp.bfloat16
I32 = jnp.int32

D_MODEL = 1024
EPS = 1e-6
NEG_INF = -1e30
ROPE_THETA = 10000.0
LOG2_E = 1.4426950408889634

MLA_HEADS = 8
MLA_Q_RANK = 256
MLA_KV_RANK = 256
MLA_NOPE = 128
MLA_ROPE = 64
MLA_V = 128
MLA_QK = MLA_NOPE + MLA_ROPE
MLA_HEAD_LANES = 256

SWA_Q_HEADS = 16
SWA_KV_HEADS = 4
SWA_GROUP = 4
SWA_HEAD_DIM = 64
SWA_WINDOW = 128
SWA_BLOCK = 128
SWA_LOOKAHEAD = 2
SWA_STEP_BLOCKS = 4

FFN_DIM = 2816
N_EXPERTS = 8
EXPERT_DIM = 2048

LANES = 128
VMEM_LIMIT = 48 * 1024 * 1024

ROW_TILE = 512
ATTN_Q_TILE = 512
ATTN_STEP_HEADS = 2
ATTN_LOOKAHEAD = 2
MOE_CHUNK = 512
MOE_GRANULE = 16
MOE_CHUNK_ROWS = 2 * MOE_CHUNK + N_EXPERTS * MOE_GRANULE
MOE_GRANULES = MOE_CHUNK_ROWS // MOE_GRANULE
MOE_DMA_UNROLL = 8
MOE_ROW_TILE = 512
MOE_TILE_GRANULES = MOE_ROW_TILE // MOE_GRANULE


def _params(*sem, flags=None):
    return pltpu.CompilerParams(dimension_semantics=sem, vmem_limit_bytes=VMEM_LIMIT, flags=flags)


def _rms(x, g):
    return x * lax.rsqrt(jnp.mean(x * x, axis=-1, keepdims=True) + EPS) * g


def _dot(a, b):
    return jnp.dot(a, b, preferred_element_type=F32)


def _dot_nt(a, b):
    return lax.dot_general(a, b, (((1,), (1,)), ((), ())), preferred_element_type=F32)


def _mla_proj_kernel(h_ref, g_ref, wd_ref, qn_ref, kvn_ref, wuq_ref, wuk_ref, wuvt_ref, cs_ref,
                     q_ref, k_ref, vt_ref):
    xn = _rms(h_ref[...], g_ref[...]).astype(BF16)
    down = _dot(xn, wd_ref[...])
    cq = _rms(down[:, :MLA_Q_RANK], qn_ref[...]).astype(BF16)
    ckv = _rms(down[:, MLA_Q_RANK:MLA_Q_RANK + MLA_KV_RANK], kvn_ref[...]).astype(BF16)
    cs = cs_ref[...]
    lane = lax.broadcasted_iota(I32, cs.shape, 1)

    def rope(a):
        p = a * cs
        return p + pltpu.roll(p, MLA_ROPE, 1)

    kr = jnp.where(lane < MLA_ROPE, rope(down[:, 2 * MLA_Q_RANK:]), 0.0)
    q = _dot(cq, wuq_ref[...]) * (MLA_QK ** -0.5 * LOG2_E)
    for h in range(MLA_HEADS):
        c = h * MLA_HEAD_LANES
        q_ref[:, c:c + LANES] = q[:, c:c + LANES].astype(BF16)
        q_ref[:, c + LANES:c + 2 * LANES] = rope(q[:, c + LANES:c + 2 * LANES]).astype(BF16)
    nk = MLA_HEADS * MLA_NOPE
    k_ref[:, :nk] = _dot(ckv, wuk_ref[...]).astype(BF16)
    k_ref[:, nk:] = kr.astype(BF16)
    vt_ref[...] = _dot_nt(wuvt_ref[...], ckv).astype(BF16)


def _mla_proj(h, batch, seq, gain, wd, qn, kvn, wuq, wuk, wuvt, cs):
    n = h.shape[0]
    t = ROW_TILE
    sb = seq // t
    full = lambda a: pl.BlockSpec(a.shape, lambda i: (0, 0))
    return pl.pallas_call(
        _mla_proj_kernel,
        grid=(n // t,),
        in_specs=[pl.BlockSpec((t, D_MODEL), lambda i: (i, 0)), full(gain), full(wd), full(qn), full(kvn),
                  full(wuq), full(wuk), full(wuvt), pl.BlockSpec((t, LANES), lambda i: (i % sb, 0))],
        out_specs=[pl.BlockSpec((t, MLA_HEADS * MLA_HEAD_LANES), lambda i: (i, 0)),
                   pl.BlockSpec((t, MLA_HEADS * MLA_NOPE + LANES), lambda i: (i, 0)),
                   pl.BlockSpec((None, MLA_HEADS * MLA_V, t), lambda i: (i // sb, 0, i % sb))],
        out_shape=[jax.ShapeDtypeStruct((n, MLA_HEADS * MLA_HEAD_LANES), BF16),
                   jax.ShapeDtypeStruct((n, MLA_HEADS * MLA_NOPE + LANES), BF16),
                   jax.ShapeDtypeStruct((batch, MLA_HEADS * MLA_V, seq), BF16)],
        compiler_params=_params("parallel"),
        name="mla_proj",
    )(h, gain, wd, qn, kvn, wuq, wuk, wuvt, cs)


def _mla_attn_kernel(q_ref, kn_ref, kr_ref, vt_ref, o_ref, kcat_ref):
    hp = ATTN_STEP_HEADS
    for h in range(hp):
        kcat_ref[h, :, :LANES] = kn_ref[:, h * LANES:(h + 1) * LANES]
        kcat_ref[h, :, LANES:] = kr_ref[...]
    seq = q_ref.shape[0]
    tq = ATTN_Q_TILE
    work = [(h, t) for h in range(hp) for t in range(seq // tq)]

    def scores(h, t):
        return _dot_nt(kcat_ref[h], q_ref[t * tq:(t + 1) * tq, h * MLA_HEAD_LANES:(h + 1) * MLA_HEAD_LANES])

    def attend(h, st):
        pt = jnp.exp2(st - jnp.max(st, axis=0, keepdims=True))
        l = jnp.sum(pt, axis=0, keepdims=True)
        ot = _dot(vt_ref[h * MLA_V:(h + 1) * MLA_V, :], pt.astype(BF16)) / l
        return ot.T.astype(o_ref.dtype)

    ahead = ATTN_LOOKAHEAD
    pending = [scores(*w) for w in work[:ahead]]
    outs = {}
    for n, (h, t) in enumerate(work):
        if n + ahead < len(work):
            pending.append(scores(*work[n + ahead]))
        outs[h, t] = attend(h, pending.pop(0))
    o_ref[...] = jnp.concatenate(
        [jnp.concatenate([outs[h, t] for t in range(seq // tq)], axis=0) for h in range(hp)], axis=1)


def _mla_attn(q, k, vt, batch, seq):
    n = q.shape[0]
    hp = ATTN_STEP_HEADS
    return pl.pallas_call(
        _mla_attn_kernel,
        grid=(batch, MLA_HEADS // hp),
        in_specs=[pl.BlockSpec((seq, hp * MLA_HEAD_LANES), lambda b, h: (b, h)),
                  pl.BlockSpec((seq, hp * LANES), lambda b, h: (b, h)),
                  pl.BlockSpec((seq, LANES), lambda b, h: (b, MLA_HEADS)),
                  pl.BlockSpec((None, hp * MLA_V, seq), lambda b, h: (b, h, 0))],
        out_specs=pl.BlockSpec((seq, hp * MLA_V), lambda b, h: (b, h)),
        out_shape=jax.ShapeDtypeStruct((n, MLA_HEADS * MLA_V), BF16),
        scratch_shapes=[pltpu.VMEM((hp, seq, MLA_HEAD_LANES), BF16)],
        compiler_params=_params("parallel", "arbitrary"),
        name="mla_attn",
    )(q, k, k, vt)


def _ffn_kernel(a_ref, wo_ref, h_ref, g_ref, wg_ref, wu_ref, wd_ref, o_ref):
    h1 = h_ref[...] + _dot(a_ref[...], wo_ref[...])
    xn = _rms(h1, g_ref[...]).astype(BF16)
    gate = _dot(xn, wg_ref[...])
    up = _dot(xn, wu_ref[...])
    act = (gate * jax.nn.sigmoid(gate) * up).astype(BF16)
    o_ref[...] = h1 + _dot(act, wd_ref[...])


def _resident(a):
    return pl.BlockSpec(a.shape, lambda i: (0,) * a.ndim, pipeline_mode=pl.Buffered(1))


def _ffn(a, wo, h, gain, wg, wu, wd):
    n = h.shape[0]
    t = ROW_TILE
    return pl.pallas_call(
        _ffn_kernel,
        grid=(n // t,),
        in_specs=[pl.BlockSpec((t, a.shape[1]), lambda i: (i, 0)), _resident(wo),
                  pl.BlockSpec((t, D_MODEL), lambda i: (i, 0)), _resident(gain),
                  _resident(wg), _resident(wu), _resident(wd)],
        out_specs=pl.BlockSpec((t, D_MODEL), lambda i: (i, 0)),
        out_shape=jax.ShapeDtypeStruct((n, D_MODEL), F32),
        compiler_params=_params("parallel"),
        name="ffn",
    )(a, wo, h, gain, wg, wu, wd)


def _norm_matmul_kernel(h_ref, g_ref, w_ref, o_ref):
    o_ref[...] = _dot(_rms(h_ref[...], g_ref[...]).astype(BF16), w_ref[...]).astype(o_ref.dtype)


def _norm_matmul(h, gain, w):
    n = h.shape[0]
    t = ROW_TILE
    return pl.pallas_call(
        _norm_matmul_kernel,
        grid=(n // t,),
        in_specs=[pl.BlockSpec((t, D_MODEL), lambda i: (i, 0)),
                  pl.BlockSpec(gain.shape, lambda i: (0, 0)),
                  pl.BlockSpec(w.shape, lambda i: (0, 0))],
        out_specs=pl.BlockSpec((t, w.shape[1]), lambda i: (i, 0)),
        out_shape=jax.ShapeDtypeStruct((n, w.shape[1]), BF16),
        compiler_params=_params("parallel"),
        name="norm_matmul",
    )(h, gain, w)


def _alibi_slope(head):
    return 2.0 ** (-8.0 * (head + 1) / SWA_Q_HEADS)


def _swa_attn_kernel(sink_ref, q_ref, k_ref, v_ref, o_ref):
    blk = SWA_BLOCK
    nb = k_ref.shape[0] // blk
    pair_shape = (blk, 2 * blk)
    row_iota = lax.broadcasted_iota(I32, pair_shape, 0)
    col = lax.broadcasted_iota(I32, pair_shape, 1)
    kcol = jnp.where(col < blk, col, col - blk)
    lane = lax.broadcasted_iota(I32, (blk, LANES), 1)
    lo = lane < SWA_HEAD_DIM

    def block(jb, carry):
        i = pl.program_id(1) * SWA_STEP_BLOCKS + jb
        q0 = pl.multiple_of(jb * blk, blk)
        qpos = i * blk + row_iota

        neg_dist, rows = [], []
        for j in (-1, 0, 1):
            kb = i + j
            kpos = kb * blk + kcol
            dist = jnp.abs(qpos - kpos)
            valid = (dist <= SWA_WINDOW) & (kb >= 0) & (kb < nb)
            neg_dist.append(jnp.where(valid, -dist.astype(F32), NEG_INF))
            rows.append(pl.multiple_of(jnp.clip(kb, 0, nb - 1) * blk, blk))

        def split(ref, g, r0):
            d = ref[pl.ds(r0, blk), g * LANES:(g + 1) * LANES]
            return jnp.concatenate([jnp.where(lo, d, 0), jnp.where(lo, 0, d)], axis=0)

        def scores(pair):
            h0 = 2 * pair
            qp = q_ref[pl.ds(q0, blk), pair * LANES:(pair + 1) * LANES]
            slope = jnp.where(col[0:1, :] < blk, _alibi_slope(h0) * LOG2_E, _alibi_slope(h0 + 1) * LOG2_E)
            return [_dot_nt(qp, split(k_ref, h0 // SWA_GROUP, rows[j])) + slope * neg_dist[j] for j in range(3)]

        def attend(pair, ss):
            h0 = 2 * pair
            ps, ls = [], []
            for half, head in ((0, h0), (1, h0 + 1)):
                sl = slice(half * blk, (half + 1) * blk)
                snk = sink_ref[head] * LOG2_E
                m = jnp.max(jnp.maximum(jnp.maximum(ss[0][:, sl], ss[1][:, sl]), ss[2][:, sl]),
                            axis=-1, keepdims=True)
                m = jnp.maximum(m, snk)
                p3 = [jnp.exp2(ss[j][:, sl] - m) for j in range(3)]
                ls.append(jnp.sum(p3[0] + p3[1] + p3[2], axis=-1, keepdims=True) + jnp.exp2(snk - m))
                ps.append(p3)
            o = jnp.zeros((blk, LANES), F32)
            for j in range(3):
                pj = jnp.concatenate([ps[0][j], ps[1][j]], axis=1).astype(BF16)
                o = o + _dot(pj, split(v_ref, h0 // SWA_GROUP, rows[j]))
            return (o / jnp.where(lo, ls[0], ls[1])).astype(o_ref.dtype)

        n_pairs = SWA_Q_HEADS // 2
        pending = {p: scores(p) for p in range(min(SWA_LOOKAHEAD, n_pairs))}
        outs = []
        for p in range(n_pairs):
            if p + SWA_LOOKAHEAD < n_pairs:
                pending[p + SWA_LOOKAHEAD] = scores(p + SWA_LOOKAHEAD)
            outs.append(attend(p, pending.pop(p)))
        o_ref[pl.ds(q0, blk), :] = jnp.concatenate(outs, axis=1)
        return carry

    lax.fori_loop(0, SWA_STEP_BLOCKS, block, 0)


def _swa_attn(qkv, sink, batch, seq):
    n = qkv.shape[0]
    rows = SWA_STEP_BLOCKS * SWA_BLOCK
    steps = seq // rows
    qw = SWA_Q_HEADS * SWA_HEAD_DIM
    kvw = SWA_KV_HEADS * LANES
    kblk = qw // kvw
    return pl.pallas_call(
        _swa_attn_kernel,
        grid=(batch, steps),
        in_specs=[pl.BlockSpec(memory_space=pltpu.SMEM),
                  pl.BlockSpec((rows, qw), lambda b, i: (b * steps + i, 0)),
                  pl.BlockSpec((seq, kvw), lambda b, i: (b, kblk)),
                  pl.BlockSpec((seq, kvw), lambda b, i: (b, kblk + 1))],
        out_specs=pl.BlockSpec((rows, qw), lambda b, i: (b * steps + i, 0)),
        out_shape=jax.ShapeDtypeStruct((n, qw), BF16),
        compiler_params=_params("parallel", "arbitrary"),
        name="swa_attn",
    )(sink, qkv, qkv, qkv)


def _moe_route_kernel(a_ref, wo_ref, h_ref, g_ref, wr_ref, h1_ref, xs_ref, pos_ref, gate_ref, cnt_ref):
    t = MOE_CHUNK
    h1 = h_ref[...] + _dot(a_ref[...], wo_ref[...])
    h1_ref[...] = h1
    xn = _rms(h1, g_ref[...])
    x_hi = xn.astype(BF16)
    x_lo = (xn - x_hi.astype(F32)).astype(BF16)
    hi = _dot(x_hi, wr_ref[...])
    logits = hi[:, :LANES] + hi[:, LANES:] + _dot(x_lo, wr_ref[:, :LANES])
    lt = logits.T[:N_EXPERTS]
    eio = lax.broadcasted_iota(I32, lt.shape, 0)
    m1 = jnp.max(lt, axis=0, keepdims=True)
    i1 = jnp.min(jnp.where(lt == m1, eio, N_EXPERTS), axis=0, keepdims=True)
    l2 = jnp.where(eio == i1, -jnp.inf, lt)
    m2 = jnp.max(l2, axis=0, keepdims=True)
    i2 = jnp.min(jnp.where(l2 == m2, eio, N_EXPERTS), axis=0, keepdims=True)
    e21 = jnp.exp(m2 - m1)
    g1 = 1.0 / (1.0 + e21)
    g2 = e21 * g1

    sel1 = eio == i1
    sel2 = eio == i2
    oh = jnp.where(sel1 | sel2, 1.0, 0.0)
    tr = lax.broadcasted_iota(I32, (t, t), 0)
    tc = lax.broadcasted_iota(I32, (t, t), 1)
    before = jnp.where(tr < tc, 1.0, 0.0).astype(BF16)
    rank = _dot(oh.astype(BF16), before)
    cnt = jnp.sum(oh, axis=1, keepdims=True).astype(I32)
    padded = jnp.bitwise_and(cnt + (MOE_GRANULE - 1), -MOE_GRANULE)
    offs, run = [], jnp.zeros((1, 1), I32)
    for e in range(N_EXPERTS):
        offs.append(run)
        run = run + padded[e:e + 1, :]
    off = jnp.concatenate(offs, axis=0).astype(F32)
    slot = rank + off
    pos1 = jnp.sum(jnp.where(sel1, slot, 0.0), axis=0, keepdims=True).astype(I32)
    pos2 = jnp.sum(jnp.where(sel2, slot, 0.0), axis=0, keepdims=True).astype(I32)

    rio = lax.broadcasted_iota(I32, (MOE_CHUNK_ROWS, t), 0)
    perm = jnp.where(rio == pos1, 1.0, jnp.where(rio == pos2, 1.0, 0.0)).astype(BF16)
    xs_ref[...] = _dot(perm, x_hi).astype(BF16)

    zi = jnp.zeros((N_EXPERTS - 2, t), I32)
    pos_ref[...] = jnp.concatenate([pos1, pos2, zi], axis=0)
    gate_ref[...] = jnp.concatenate([g1, g2, zi.astype(F32)], axis=0)
    cnt_ref[...] = jnp.broadcast_to(padded, (N_EXPERTS, LANES))


def _moe_route(a, wo, h, gain, wr):
    n = h.shape[0]
    t = MOE_CHUNK
    nc = n // t
    return pl.pallas_call(
        _moe_route_kernel,
        grid=(nc,),
        in_specs=[pl.BlockSpec((t, a.shape[1]), lambda i: (i, 0)),
                  pl.BlockSpec(wo.shape, lambda i: (0, 0)),
                  pl.BlockSpec((t, D_MODEL), lambda i: (i, 0)),
                  pl.BlockSpec(gain.shape, lambda i: (0, 0)),
                  pl.BlockSpec(wr.shape, lambda i: (0, 0))],
        out_specs=[pl.BlockSpec((t, D_MODEL), lambda i: (i, 0)),
                   pl.BlockSpec((MOE_CHUNK_ROWS, D_MODEL), lambda i: (i, 0)),
                   pl.BlockSpec((None, N_EXPERTS, t), lambda i: (i, 0, 0)),
                   pl.BlockSpec((None, N_EXPERTS, t), lambda i: (i, 0, 0)),
                   pl.BlockSpec((None, N_EXPERTS, LANES), lambda i: (i, 0, 0))],
        out_shape=[jax.ShapeDtypeStruct((n, D_MODEL), F32),
                   jax.ShapeDtypeStruct((nc * MOE_CHUNK_ROWS, D_MODEL), BF16),
                   jax.ShapeDtypeStruct((nc, N_EXPERTS, t), I32),
                   jax.ShapeDtypeStruct((nc, N_EXPERTS, t), F32),
                   jax.ShapeDtypeStruct((nc, N_EXPERTS, LANES), I32)],
        compiler_params=_params("parallel"),
        name="moe_route",
    )(a, wo, h, gain, wr)


def _moe_plan(padded, n_tokens):
    nc = padded.shape[0]
    g = MOE_GRANULE
    loc_off = jnp.cumsum(padded, axis=1) - padded
    total = jnp.sum(padded, axis=0)
    total_pad = ((total + MOE_ROW_TILE - 1) // MOE_ROW_TILE) * MOE_ROW_TILE
    gbase = jnp.cumsum(total_pad) - total_pad
    coff = jnp.cumsum(padded, axis=0) - padded
    shift = gbase[None, :] + coff - loc_off
    step = shift[:, 1:] - shift[:, :-1]
    ends = (loc_off + padded)[:, :-1]
    row = jnp.arange(MOE_GRANULES, dtype=I32)[None, :, None] * g
    dest = row[:, :, 0] + shift[:, :1] + jnp.sum(jnp.where(ends[:, None, :] <= row, step[:, None, :], 0), axis=2)
    n_used = jnp.sum(padded, axis=1) // g
    valid = jnp.arange(MOE_GRANULES, dtype=I32)[None, :] < n_used[:, None]
    gdest = jnp.where(valid, dest // g, 0).astype(I32).reshape(-1)
    n_tiles = _moe_tiles(n_tokens)
    tile_row = jnp.arange(n_tiles, dtype=I32) * MOE_ROW_TILE
    tile_expert = jnp.minimum(jnp.sum(((gbase + total_pad)[None, :] <= tile_row[:, None]).astype(I32), axis=1),
                              N_EXPERTS - 1).astype(I32)
    tiles_used = ((gbase[-1] + total_pad[-1]) // MOE_ROW_TILE).astype(I32).reshape(1)
    back = (jnp.arange(nc, dtype=I32)[:, None] * MOE_CHUNK_ROWS + loc_off - gbase[None, :] - coff) // g
    run_end = (gbase[None, :] + coff + padded) // g
    gran = jnp.arange(n_tiles * MOE_TILE_GRANULES, dtype=I32)
    own = (gbase[None, :] <= gran[:, None] * g) & (gran[:, None] * g < (gbase + total_pad)[None, :])
    pick = lambda tbl: jnp.sum(jnp.where(own[:, None, :], tbl[None, :, :], 0), axis=2)
    hops = jnp.sum(jnp.where(pick(run_end[:-1]) <= gran[:, None], pick(back[1:] - back[:-1]), 0), axis=1)
    src = gran + pick(back[:1])[:, 0] + hops
    filled = gran < pick(((gbase + total) // g)[None, :])[:, 0]
    src = jnp.where(filled, src, MOE_GRANULES - 1).astype(I32)
    return gdest, n_used.astype(I32), tile_expert, tiles_used, src


def _moe_tiles(n_tokens):
    rows = (n_tokens // MOE_CHUNK) * MOE_CHUNK_ROWS + N_EXPERTS * MOE_ROW_TILE
    return -(-rows // MOE_ROW_TILE)


def _granule_copy(src, dst, sem):
    return pltpu.make_async_copy(src, dst, sem)


def _moe_expert_kernel(src_ref, te_ref, nt_ref, xs_ref, wg_ref, wu_ref, wd_ref, o_ref, xbuf_ref, sem):
    del te_ref
    i = pl.program_id(0)
    n = nt_ref[0]
    g = MOE_GRANULE

    def copy(tile, k):
        s = src_ref[tile * MOE_TILE_GRANULES + k]
        return _granule_copy(xs_ref.at[pl.ds(pl.multiple_of(s * g, g), g)],
                             xbuf_ref.at[tile % 2, pl.ds(pl.multiple_of(k * g, g), g)], sem.at[tile % 2])

    def for_tile(tile, action):
        def step(k, carry):
            action(copy(tile, k))
            return carry
        lax.fori_loop(0, MOE_TILE_GRANULES, step, 0, unroll=MOE_DMA_UNROLL)

    @pl.when(i == 0)
    def _():
        for_tile(i, lambda cp: cp.start())

    @pl.when(i + 1 < n)
    def _():
        for_tile(i + 1, lambda cp: cp.start())

    @pl.when(i < n)
    def _():
        for_tile(i, lambda cp: cp.wait())
        x = xbuf_ref[i % 2]
        gate = _dot(x, wg_ref[...])
        up = _dot(x, wu_ref[...])
        act = (gate * jax.nn.sigmoid(gate) * up).astype(BF16)
        o_ref[...] = _dot(act, wd_ref[...]).astype(o_ref.dtype)

    @pl.when(i >= n)
    def _():
        o_ref[...] = jnp.zeros_like(o_ref)


def _moe_expert(src, tile_expert, tiles_used, xs, wg, wu, wd):
    nt = tile_expert.shape[0]

    def expert(i, src_ref, te_ref, nt_ref):
        return te_ref[jnp.minimum(i, nt_ref[0] - 1)]

    return pl.pallas_call(
        _moe_expert_kernel,
        grid_spec=pltpu.PrefetchScalarGridSpec(
            num_scalar_prefetch=3,
            grid=(nt,),
            in_specs=[pl.BlockSpec(memory_space=pl.ANY),
                      pl.BlockSpec((None, D_MODEL, EXPERT_DIM), lambda i, *p: (expert(i, *p), 0, 0)),
                      pl.BlockSpec((None, D_MODEL, EXPERT_DIM), lambda i, *p: (expert(i, *p), 0, 0)),
                      pl.BlockSpec((None, EXPERT_DIM, D_MODEL), lambda i, *p: (expert(i, *p), 0, 0))],
            out_specs=pl.BlockSpec((MOE_ROW_TILE, D_MODEL), lambda i, *p: (i, 0)),
            scratch_shapes=[pltpu.VMEM((2, MOE_ROW_TILE, D_MODEL), BF16), pltpu.SemaphoreType.DMA((2,))]),
        out_shape=jax.ShapeDtypeStruct((nt * MOE_ROW_TILE, D_MODEL), BF16),
        compiler_params=_params("arbitrary"),
        name="moe_expert",
    )(src, tile_expert, tiles_used, xs, wg, wu, wd)


def _moe_combine_kernel(gd_ref, nu_ref, h_ref, pos_ref, gate_ref, fg_ref, ys_ref, o_ref, ybuf_ref, sem,
                        *, final_norm):
    c = pl.program_id(0)
    nc = pl.num_programs(0)
    g = MOE_GRANULE
    t = MOE_CHUNK
    slot = c % 2

    def copy(chunk, j):
        d = gd_ref[chunk * MOE_GRANULES + j]
        return _granule_copy(ys_ref.at[pl.ds(pl.multiple_of(d * g, g), g)],
                             ybuf_ref.at[chunk % 2, pl.ds(pl.multiple_of(j * g, g), g)], sem.at[chunk % 2])

    def for_used(chunk, action):
        def step(j, carry):
            @pl.when(j < nu_ref[chunk])
            def _():
                action(copy(chunk, j))
            return carry
        lax.fori_loop(0, MOE_GRANULES, step, 0, unroll=MOE_DMA_UNROLL)

    @pl.when(c == 0)
    def _():
        ybuf_ref[...] = jnp.zeros_like(ybuf_ref)
        for_used(c, lambda cp: cp.start())

    @pl.when(c + 1 < nc)
    def _():
        for_used(c + 1, lambda cp: cp.start())

    for_used(c, lambda cp: cp.wait())

    rio = lax.broadcasted_iota(I32, (MOE_CHUNK_ROWS, t), 0)
    w = (jnp.where(rio == pos_ref[0:1, :], gate_ref[0:1, :], 0.0)
         + jnp.where(rio == pos_ref[1:2, :], gate_ref[1:2, :], 0.0)).astype(BF16)
    y = lax.dot_general(w, ybuf_ref[slot], (((0,), (0,)), ((), ())), preferred_element_type=F32)
    out = h_ref[...] + y
    if final_norm:
        out = _rms(out, fg_ref[...])
    o_ref[...] = out


def _moe_combine(gdest, n_used, h, pos, gates, ys, final_gain, final_norm):
    n = h.shape[0]
    t = MOE_CHUNK
    nc = n // t
    return pl.pallas_call(
        functools.partial(_moe_combine_kernel, final_norm=final_norm),
        grid_spec=pltpu.PrefetchScalarGridSpec(
            num_scalar_prefetch=2,
            grid=(nc,),
            in_specs=[pl.BlockSpec((t, D_MODEL), lambda i, gd, nu: (i, 0)),
                      pl.BlockSpec((None, N_EXPERTS, t), lambda i, gd, nu: (i, 0, 0)),
                      pl.BlockSpec((None, N_EXPERTS, t), lambda i, gd, nu: (i, 0, 0)),
                      pl.BlockSpec(final_gain.shape, lambda i, gd, nu: (0, 0)),
                      pl.BlockSpec(memory_space=pl.ANY)],
            out_specs=pl.BlockSpec((t, D_MODEL), lambda i, gd, nu: (i, 0)),
            scratch_shapes=[pltpu.VMEM((2, MOE_CHUNK_ROWS, D_MODEL), BF16), pltpu.SemaphoreType.DMA((2,))]),
        out_shape=jax.ShapeDtypeStruct((n, D_MODEL), F32),
        compiler_params=_params("arbitrary"),
        name="moe_combine",
    )(gdest, n_used, h, pos, gates, final_gain, ys)


def _moe(a, wo, h, gain, router, wg, wu, wd, final_gain, final_norm):
    n = h.shape[0]
    wr = jnp.pad(router, ((0, 0), (0, LANES - N_EXPERTS)))
    wr_hi = wr.astype(BF16)
    wr_lo = (wr - wr_hi.astype(F32)).astype(BF16)
    h, xs, pos, gates, cnt = _moe_route(a, wo, h, gain, jnp.concatenate([wr_hi, wr_lo], axis=1))
    gdest, n_used, tile_expert, tiles_used, src = _moe_plan(cnt[:, :, 0], n)
    ys = _moe_expert(src, tile_expert, tiles_used, xs, wg.astype(BF16), wu.astype(BF16), wd.astype(BF16))
    return _moe_combine(gdest, n_used, h, pos, gates, ys, final_gain, final_norm)


def _rot_cols(w):
    half = MLA_ROPE // 2
    return jnp.concatenate([-w[..., half:], w[..., :half]], axis=-1)


def _rope_table(seq):
    inv = 1.0 / (ROPE_THETA ** (jnp.arange(0, MLA_ROPE, 2, dtype=F32) / MLA_ROPE))
    ang = jnp.arange(seq, dtype=F32)[:, None] * inv[None, :]
    cos, sin = jnp.cos(ang), jnp.sin(ang)
    return jnp.concatenate([cos, cos, sin, sin], axis=-1)


def _mla_weights(w_dqkv, w_uq, w_uk, w_uv):
    lat = MLA_Q_RANK + MLA_KV_RANK
    rope = w_dqkv[:, lat:]
    wd = jnp.concatenate([w_dqkv[:, :lat], rope, _rot_cols(rope)], axis=1).astype(BF16)
    q_rope = w_uq[..., MLA_NOPE:]
    wuq = jnp.concatenate([w_uq[..., :MLA_NOPE], q_rope, _rot_cols(q_rope)], axis=-1)
    wuq = wuq.reshape(MLA_Q_RANK, MLA_HEADS * MLA_HEAD_LANES).astype(BF16)
    wuk = w_uk.reshape(MLA_KV_RANK, -1).astype(BF16)
    wuvt = w_uv.reshape(MLA_KV_RANK, -1).T.astype(BF16)
    return wd, wuq, wuk, wuvt


def _swa_weights(w_qkv):
    qw = SWA_Q_HEADS * SWA_HEAD_DIM
    kvw = SWA_KV_HEADS * SWA_HEAD_DIM
    dup = lambda w: jnp.concatenate([w.reshape(D_MODEL, SWA_KV_HEADS, 1, SWA_HEAD_DIM)] * 2, axis=2).reshape(D_MODEL, -1)
    wq = w_qkv[:, :qw] * (SWA_HEAD_DIM ** -0.5 * LOG2_E)
    return jnp.concatenate([wq, dup(w_qkv[:, qw:qw + kvw]), dup(w_qkv[:, qw + kvw:])], axis=1).astype(BF16)


def kernel(x, mla_norm, mla_w_dqkv, mla_q_norm, mla_w_uq, mla_kv_norm, mla_w_uk, mla_w_uv, mla_w_o, swa_norm, swa_w_qkv, swa_sink, swa_w_o, ffn_norm, ffn_w_gate, ffn_w_up, ffn_w_down, moe_norm, moe_router, moe_w_gate, moe_w_up, moe_w_down, final_norm):
    batch, seq, _ = x.shape
    h = x.reshape(batch * seq, D_MODEL)
    cs = _rope_table(seq)
    row = lambda v: v.reshape(1, -1).astype(F32)
    depth = 2 * mla_norm.shape[0]
    for layer in range(depth):
        j = layer // 2
        if layer % 2 == 0:
            wd, wuq, wuk, wuvt = _mla_weights(mla_w_dqkv[j], mla_w_uq[j], mla_w_uk[j], mla_w_uv[j])
            q, k, vt = _mla_proj(h, batch, seq, row(mla_norm[j]), wd, row(mla_q_norm[j]), row(mla_kv_norm[j]),
                                 wuq, wuk, wuvt, cs)
            a = _mla_attn(q, k, vt, batch, seq)
            h = _ffn(a, mla_w_o[j].astype(BF16), h, row(ffn_norm[j]), ffn_w_gate[j].astype(BF16),
                     ffn_w_up[j].astype(BF16), ffn_w_down[j].astype(BF16))
        else:
            qkv = _norm_matmul(h, row(swa_norm[j]), _swa_weights(swa_w_qkv[j]))
            a = _swa_attn(qkv, swa_sink[j].astype(F32), batch, seq)
            h = _moe(a, swa_w_o[j].astype(BF16), h, row(moe_norm[j]), moe_router[j], moe_w_gate[j], moe_w_up[j],
                     moe_w_down[j], row(final_norm), layer == depth - 1)
    return h.reshape(batch, seq, D_MODEL)
```

```python
import functools

import jax
import jax.numpy as jnp
from jax import lax
from jax.experimental import pallas as pl
from jax.experimental.pallas import tpu as pltpu

F32 = jnp.float32
BF16 = jnp.bfloat16
I32 = jnp.int32

D_MODEL = 1024
EPS = 1e-6
NEG_INF = -1e30
ROPE_THETA = 10000.0
LOG2_E = 1.4426950408889634

MLA_HEADS = 8
MLA_Q_RANK = 256
MLA_KV_RANK = 256
MLA_NOPE = 128
MLA_ROPE = 64
MLA_V = 128
MLA_QK = MLA_NOPE + MLA_ROPE
MLA_HEAD_LANES = 256

SWA_Q_HEADS = 16
SWA_KV_HEADS = 4
SWA_GROUP = 4
SWA_HEAD_DIM = 64
SWA_WINDOW = 128
SWA_BLOCK = 128
SWA_LOOKAHEAD = 2
SWA_STEP_BLOCKS = 4

FFN_DIM = 2816
N_EXPERTS = 8
EXPERT_DIM = 2048

LANES = 128
VMEM_LIMIT = 48 * 1024 * 1024

ROW_TILE = 512
ATTN_Q_TILE = 512
ATTN_STEP_HEADS = 2
ATTN_LOOKAHEAD = 2
MOE_CHUNK = 512
MOE_GRANULE = 16
MOE_CHUNK_ROWS = 2 * MOE_CHUNK + N_EXPERTS * MOE_GRANULE
MOE_GRANULES = MOE_CHUNK_ROWS // MOE_GRANULE
MOE_DMA_UNROLL = 8
MOE_ROW_TILE = 512
MOE_TILE_GRANULES = MOE_ROW_TILE // MOE_GRANULE


def _params(*sem, flags=None):
    return pltpu.CompilerParams(dimension_semantics=sem, vmem_limit_bytes=VMEM_LIMIT, flags=flags)


def _rms(x, g):
    return x * lax.rsqrt(jnp.mean(x * x, axis=-1, keepdims=True) + EPS) * g


def _dot(a, b):
    return jnp.dot(a, b, preferred_element_type=F32)


def _dot_nt(a, b):
    return lax.dot_general(a, b, (((1,), (1,)), ((), ())), preferred_element_type=F32)


def _mla_proj_kernel(h_ref, g_ref, wd_ref, qn_ref, kvn_ref, wuq_ref, wuk_ref, wuvt_ref, cs_ref,
                     q_ref, k_ref, vt_ref):
    xn = _rms(h_ref[...], g_ref[...]).astype(BF16)
    down = _dot(xn, wd_ref[...])
    cq = _rms(down[:, :MLA_Q_RANK], qn_ref[...]).astype(BF16)
    ckv = _rms(down[:, MLA_Q_RANK:MLA_Q_RANK + MLA_KV_RANK], kvn_ref[...]).astype(BF16)
    cs = cs_ref[...]
    lane = lax.broadcasted_iota(I32, cs.shape, 1)

    def rope(a):
        p = a * cs
        return p + pltpu.roll(p, MLA_ROPE, 1)

    kr = jnp.where(lane < MLA_ROPE, rope(down[:, 2 * MLA_Q_RANK:]), 0.0)
    q = _dot(cq, wuq_ref[...]) * (MLA_QK ** -0.5 * LOG2_E)
    for h in range(MLA_HEADS):
        c = h * MLA_HEAD_LANES
        q_ref[:, c:c + LANES] = q[:, c:c + LANES].astype(BF16)
        q_ref[:, c + LANES:c + 2 * LANES] = rope(q[:, c + LANES:c + 2 * LANES]).astype(BF16)
    nk = MLA_HEADS * MLA_NOPE
    k_ref[:, :nk] = _dot(ckv, wuk_ref[...]).astype(BF16)
    k_ref[:, nk:] = kr.astype(BF16)
    vt_ref[...] = _dot_nt(wuvt_ref[...], ckv).astype(BF16)


def _mla_proj(h, batch, seq, gain, wd, qn, kvn, wuq, wuk, wuvt, cs):
    n = h.shape[0]
    t = ROW_TILE
    sb = seq // t
    full = lambda a: pl.BlockSpec(a.shape, lambda i: (0, 0))
    return pl.pallas_call(
        _mla_proj_kernel,
        grid=(n // t,),
        in_specs=[pl.BlockSpec((t, D_MODEL), lambda i: (i, 0)), full(gain), full(wd), full(qn), full(kvn),
                  full(wuq), full(wuk), full(wuvt), pl.BlockSpec((t, LANES), lambda i: (i % sb, 0))],
        out_specs=[pl.BlockSpec((t, MLA_HEADS * MLA_HEAD_LANES), lambda i: (i, 0)),
                   pl.BlockSpec((t, MLA_HEADS * MLA_NOPE + LANES), lambda i: (i, 0)),
                   pl.BlockSpec((None, MLA_HEADS * MLA_V, t), lambda i: (i // sb, 0, i % sb))],
        out_shape=[jax.ShapeDtypeStruct((n, MLA_HEADS * MLA_HEAD_LANES), BF16),
                   jax.ShapeDtypeStruct((n, MLA_HEADS * MLA_NOPE + LANES), BF16),
                   jax.ShapeDtypeStruct((batch, MLA_HEADS * MLA_V, seq), BF16)],
        compiler_params=_params("parallel"),
        name="mla_proj",
    )(h, gain, wd, qn, kvn, wuq, wuk, wuvt, cs)


def _mla_attn_kernel(q_ref, kn_ref, kr_ref, vt_ref, o_ref, kcat_ref):
    hp = ATTN_STEP_HEADS
    for h in range(hp):
        kcat_ref[h, :, :LANES] = kn_ref[:, h * LANES:(h + 1) * LANES]
        kcat_ref[h, :, LANES:] = kr_ref[...]
    seq = q_ref.shape[0]
    tq = ATTN_Q_TILE
    work = [(h, t) for h in range(hp) for t in range(seq // tq)]

    def scores(h, t):
        return _dot_nt(kcat_ref[h], q_ref[t * tq:(t + 1) * tq, h * MLA_HEAD_LANES:(h + 1) * MLA_HEAD_LANES])

    def attend(h, st):
        pt = jnp.exp2(st - jnp.max(st, axis=0, keepdims=True))
        l = jnp.sum(pt, axis=0, keepdims=True)
        ot = _dot(vt_ref[h * MLA_V:(h + 1) * MLA_V, :], pt.astype(BF16)) / l
        return ot.T.astype(o_ref.dtype)

    ahead = ATTN_LOOKAHEAD
    pending = [scores(*w) for w in work[:ahead]]
    outs = {}
    for n, (h, t) in enumerate(work):
        if n + ahead < len(work):
            pending.append(scores(*work[n + ahead]))
        outs[h, t] = attend(h, pending.pop(0))
    o_ref[...] = jnp.concatenate(
        [jnp.concatenate([outs[h, t] for t in range(seq // tq)], axis=0) for h in range(hp)], axis=1)


def _mla_attn(q, k, vt, batch, seq):
    n = q.shape[0]
    hp = ATTN_STEP_HEADS
    return pl.pallas_call(
        _mla_attn_kernel,
        grid=(batch, MLA_HEADS // hp),
        in_specs=[pl.BlockSpec((seq, hp * MLA_HEAD_LANES), lambda b, h: (b, h)),
                  pl.BlockSpec((seq, hp * LANES), lambda b, h: (b, h)),
                  pl.BlockSpec((seq, LANES), lambda b, h: (b, MLA_HEADS)),
                  pl.BlockSpec((None, hp * MLA_V, seq), lambda b, h: (b, h, 0))],
        out_specs=pl.BlockSpec((seq, hp * MLA_V), lambda b, h: (b, h)),
        out_shape=jax.ShapeDtypeStruct((n, MLA_HEADS * MLA_V), BF16),
        scratch_shapes=[pltpu.VMEM((hp, seq, MLA_HEAD_LANES), BF16)],
        compiler_params=_params("parallel", "arbitrary"),
        name="mla_attn",
    )(q, k, k, vt)


def _ffn_kernel(a_ref, wo_ref, h_ref, g_ref, wg_ref, wu_ref, wd_ref, o_ref):
    h1 = h_ref[...] + _dot(a_ref[...], wo_ref[...])
    xn = _rms(h1, g_ref[...]).astype(BF16)
    gate = _dot(xn, wg_ref[...])
    up = _dot(xn, wu_ref[...])
    act = (gate * jax.nn.sigmoid(gate) * up).astype(BF16)
    o_ref[...] = h1 + _dot(act, wd_ref[...])


def _resident(a):
    return pl.BlockSpec(a.shape, lambda i: (0,) * a.ndim, pipeline_mode=pl.Buffered(1))


def _ffn(a, wo, h, gain, wg, wu, wd):
    n = h.shape[0]
    t = ROW_TILE
    return pl.pallas_call(
        _ffn_kernel,
        grid=(n // t,),
        in_specs=[pl.BlockSpec((t, a.shape[1]), lambda i: (i, 0)), _resident(wo),
                  pl.BlockSpec((t, D_MODEL), lambda i: (i, 0)), _resident(gain),
                  _resident(wg), _resident(wu), _resident(wd)],
        out_specs=pl.BlockSpec((t, D_MODEL), lambda i: (i, 0)),
        out_shape=jax.ShapeDtypeStruct((n, D_MODEL), F32),
        compiler_params=_params("parallel"),
        name="ffn",
    )(a, wo, h, gain, wg, wu, wd)


def _norm_matmul_kernel(h_ref, g_ref, w_ref, o_ref):
    o_ref[...] = _dot(_rms(h_ref[...], g_ref[...]).astype(BF16), w_ref[...]).astype(o_ref.dtype)


def _norm_matmul(h, gain, w):
    n = h.shape[0]
    t = ROW_TILE
    return pl.pallas_call(
        _norm_matmul_kernel,
        grid=(n // t,),
        in_specs=[pl.BlockSpec((t, D_MODEL), lambda i: (i, 0)),
                  pl.BlockSpec(gain.shape, lambda i: (0, 0)),
                  pl.BlockSpec(w.shape, lambda i: (0, 0))],
        out_specs=pl.BlockSpec((t, w.shape[1]), lambda i: (i, 0)),
        out_shape=jax.ShapeDtypeStruct((n, w.shape[1]), BF16),
        compiler_params=_params("parallel"),
        name="norm_matmul",
    )(h, gain, w)


def _alibi_slope(head):
    return 2.0 ** (-8.0 * (head + 1) / SWA_Q_HEADS)


def _swa_bias_table():
    t = jnp.arange(SWA_BLOCK, dtype=I32)[:, None]
    s = jnp.arange(SWA_BLOCK, dtype=I32)[None, :]
    slots = []
    for j in (-1, 0, 1):
        dist = jnp.abs(t - (s + j * SWA_BLOCK))
        slots.append(jnp.where(dist <= SWA_WINDOW, -dist.astype(F32), NEG_INF))
    slots.append(jnp.full((SWA_BLOCK, SWA_BLOCK), NEG_INF, F32))
    base = jnp.stack(slots)
    slopes = jnp.asarray([_alibi_slope(h) * LOG2_E for h in range(SWA_Q_HEADS)], F32)
    per_head = slopes[:, None, None, None] * base[None]
    return jnp.concatenate([per_head[0::2], per_head[1::2]], axis=-1)


def _swa_attn_kernel(sink_ref, q_ref, k_ref, v_ref, bias_ref, o_ref, klo_ref, khi_ref, vlo_ref, vhi_ref):
    blk = SWA_BLOCK
    nb = k_ref.shape[0] // blk
    lane = lax.broadcasted_iota(I32, (blk, LANES), 1)
    lo = lane < SWA_HEAD_DIM

    @pl.when(pl.program_id(1) == 0)
    def _():
        keep = lax.broadcasted_iota(I32, k_ref.shape, 1) % LANES < SWA_HEAD_DIM
        klo_ref[...] = jnp.where(keep, k_ref[...], 0)
        khi_ref[...] = jnp.where(keep, 0, k_ref[...])
        vlo_ref[...] = jnp.where(keep, v_ref[...], 0)
        vhi_ref[...] = jnp.where(keep, 0, v_ref[...])

    def block(jb, carry):
        i = pl.program_id(1) * SWA_STEP_BLOCKS + jb
        q0 = pl.multiple_of(jb * blk, blk)
        rows, slots = [], []
        for j in (-1, 0, 1):
            kb = i + j
            rows.append(pl.multiple_of(jnp.clip(kb, 0, nb - 1) * blk, blk))
            slots.append(jnp.where((kb >= 0) & (kb < nb), j + 1, 3))

        def split(lo_ref, hi_ref, g, r0):
            return jnp.concatenate([lo_ref[pl.ds(r0, blk), g * LANES:(g + 1) * LANES],
                                    hi_ref[pl.ds(r0, blk), g * LANES:(g + 1) * LANES]], axis=0)

        def scores(pair):
            g = 2 * pair // SWA_GROUP
            qp = q_ref[pl.ds(q0, blk), pair * LANES:(pair + 1) * LANES]
            return [_dot_nt(qp, split(klo_ref, khi_ref, g, rows[j])) + bias_ref[pair, slots[j]] for j in range(3)]

        def attend(pair, ss):
            h0 = 2 * pair
            ps, ls = [], []
            for half, head in ((0, h0), (1, h0 + 1)):
                sl = slice(half * blk, (half + 1) * blk)
                snk = sink_ref[head] * LOG2_E
                m = jnp.max(jnp.maximum(jnp.maximum(ss[0][:, sl], ss[1][:, sl]), ss[2][:, sl]),
                            axis=-1, keepdims=True)
                m = jnp.maximum(m, snk)
                p3 = [jnp.exp2(ss[j][:, sl] - m) for j in range(3)]
                ls.append(jnp.sum(p3[0] + p3[1] + p3[2], axis=-1, keepdims=True) + jnp.exp2(snk - m))
                ps.append(p3)
            o = jnp.zeros((blk, LANES), F32)
            for j in range(3):
                pj = jnp.concatenate([ps[0][j], ps[1][j]], axis=1).astype(BF16)
                o = o + _dot(pj, split(vlo_ref, vhi_ref, h0 // SWA_GROUP, rows[j]))
            return (o / jnp.where(lo, ls[0], ls[1])).astype(o_ref.dtype)

        n_pairs = SWA_Q_HEADS // 2
        pending = {p: scores(p) for p in range(min(SWA_LOOKAHEAD, n_pairs))}
        outs = []
        for p in range(n_pairs):
            if p + SWA_LOOKAHEAD < n_pairs:
                pending[p + SWA_LOOKAHEAD] = scores(p + SWA_LOOKAHEAD)
            outs.append(attend(p, pending.pop(p)))
        o_ref[pl.ds(q0, blk), :] = jnp.concatenate(outs, axis=1)
        return carry

    lax.fori_loop(0, SWA_STEP_BLOCKS, block, 0)


def _swa_attn(qkv, sink, bias, batch, seq):
    n = qkv.shape[0]
    rows = SWA_STEP_BLOCKS * SWA_BLOCK
    steps = seq // rows
    qw = SWA_Q_HEADS * SWA_HEAD_DIM
    kvw = SWA_KV_HEADS * LANES
    kblk = qw // kvw
    return pl.pallas_call(
        _swa_attn_kernel,
        grid=(batch, steps),
        in_specs=[pl.BlockSpec(memory_space=pltpu.SMEM),
                  pl.BlockSpec((rows, qw), lambda b, i: (b * steps + i, 0)),
                  pl.BlockSpec((seq, kvw), lambda b, i: (b, kblk)),
                  pl.BlockSpec((seq, kvw), lambda b, i: (b, kblk + 1)),
                  pl.BlockSpec(bias.shape, lambda b, i: (0, 0, 0, 0), pipeline_mode=pl.Buffered(1))],
        out_specs=pl.BlockSpec((rows, qw), lambda b, i: (b * steps + i, 0)),
        out_shape=jax.ShapeDtypeStruct((n, qw), BF16),
        scratch_shapes=[pltpu.VMEM((seq, kvw), BF16) for _ in range(4)],
        compiler_params=_params("arbitrary", "arbitrary"),
        name="swa_attn",
    )(sink, qkv, qkv, qkv, bias)


def _moe_route_kernel(a_ref, wo_ref, h_ref, g_ref, wr_ref, h1_ref, xs_ref, pos_ref, gate_ref, cnt_ref):
    t = MOE_CHUNK
    h1 = h_ref[...] + _dot(a_ref[...], wo_ref[...])
    h1_ref[...] = h1
    xn = _rms(h1, g_ref[...])
    x_hi = xn.astype(BF16)
    x_lo = (xn - x_hi.astype(F32)).astype(BF16)
    hi = _dot(x_hi, wr_ref[...])
    logits = hi[:, :LANES] + hi[:, LANES:] + _dot(x_lo, wr_ref[:, :LANES])
    lt = logits.T[:N_EXPERTS]
    eio = lax.broadcasted_iota(I32, lt.shape, 0)
    m1 = jnp.max(lt, axis=0, keepdims=True)
    i1 = jnp.min(jnp.where(lt == m1, eio, N_EXPERTS), axis=0, keepdims=True)
    l2 = jnp.where(eio == i1, -jnp.inf, lt)
    m2 = jnp.max(l2, axis=0, keepdims=True)
    i2 = jnp.min(jnp.where(l2 == m2, eio, N_EXPERTS), axis=0, keepdims=True)
    e21 = jnp.exp(m2 - m1)
    g1 = 1.0 / (1.0 + e21)
    g2 = e21 * g1

    sel1 = eio == i1
    sel2 = eio == i2
    oh = jnp.where(sel1 | sel2, 1.0, 0.0)
    tr = lax.broadcasted_iota(I32, (t, t), 0)
    tc = lax.broadcasted_iota(I32, (t, t), 1)
    before = jnp.where(tr < tc, 1.0, 0.0).astype(BF16)
    rank = _dot(oh.astype(BF16), before)
    cnt = jnp.sum(oh, axis=1, keepdims=True).astype(I32)
    padded = jnp.bitwise_and(cnt + (MOE_GRANULE - 1), -MOE_GRANULE)
    offs, run = [], jnp.zeros((1, 1), I32)
    for e in range(N_EXPERTS):
        offs.append(run)
        run = run + padded[e:e + 1, :]
    off = jnp.concatenate(offs, axis=0).astype(F32)
    slot = rank + off
    pos1 = jnp.sum(jnp.where(sel1, slot, 0.0), axis=0, keepdims=True).astype(I32)
    pos2 = jnp.sum(jnp.where(sel2, slot, 0.0), axis=0, keepdims=True).astype(I32)

    rio = lax.broadcasted_iota(I32, (MOE_CHUNK_ROWS, t), 0)
    perm = jnp.where(rio == pos1, 1.0, jnp.where(rio == pos2, 1.0, 0.0)).astype(BF16)
    xs_ref[...] = _dot(perm, x_hi).astype(BF16)

    zi = jnp.zeros((N_EXPERTS - 2, t), I32)
    pos_ref[...] = jnp.concatenate([pos1, pos2, zi], axis=0)
    gate_ref[...] = jnp.concatenate([g1, g2, zi.astype(F32)], axis=0)
    cnt_ref[...] = jnp.broadcast_to(padded, (N_EXPERTS, LANES))


def _moe_route(a, wo, h, gain, wr):
    n = h.shape[0]
    t = MOE_CHUNK
    nc = n // t
    return pl.pallas_call(
        _moe_route_kernel,
        grid=(nc,),
        in_specs=[pl.BlockSpec((t, a.shape[1]), lambda i: (i, 0)),
                  pl.BlockSpec(wo.shape, lambda i: (0, 0)),
                  pl.BlockSpec((t, D_MODEL), lambda i: (i, 0)),
                  pl.BlockSpec(gain.shape, lambda i: (0, 0)),
                  pl.BlockSpec(wr.shape, lambda i: (0, 0))],
        out_specs=[pl.BlockSpec((t, D_MODEL), lambda i: (i, 0)),
                   pl.BlockSpec((MOE_CHUNK_ROWS, D_MODEL), lambda i: (i, 0)),
                   pl.BlockSpec((None, N_EXPERTS, t), lambda i: (i, 0, 0)),
                   pl.BlockSpec((None, N_EXPERTS, t), lambda i: (i, 0, 0)),
                   pl.BlockSpec((None, N_EXPERTS, LANES), lambda i: (i, 0, 0))],
        out_shape=[jax.ShapeDtypeStruct((n, D_MODEL), F32),
                   jax.ShapeDtypeStruct((nc * MOE_CHUNK_ROWS, D_MODEL), BF16),
                   jax.ShapeDtypeStruct((nc, N_EXPERTS, t), I32),
                   jax.ShapeDtypeStruct((nc, N_EXPERTS, t), F32),
                   jax.ShapeDtypeStruct((nc, N_EXPERTS, LANES), I32)],
        compiler_params=_params("parallel"),
        name="moe_route",
    )(a, wo, h, gain, wr)


def _moe_plan(padded, n_tokens):
    nc = padded.shape[0]
    g = MOE_GRANULE
    loc_off = jnp.cumsum(padded, axis=1) - padded
    total = jnp.sum(padded, axis=0)
    total_pad = ((total + MOE_ROW_TILE - 1) // MOE_ROW_TILE) * MOE_ROW_TILE
    gbase = jnp.cumsum(total_pad) - total_pad
    coff = jnp.cumsum(padded, axis=0) - padded
    shift = gbase[None, :] + coff - loc_off
    step = shift[:, 1:] - shift[:, :-1]
    ends = (loc_off + padded)[:, :-1]
    row = jnp.arange(MOE_GRANULES, dtype=I32)[None, :, None] * g
    dest = row[:, :, 0] + shift[:, :1] + jnp.sum(jnp.where(ends[:, None, :] <= row, step[:, None, :], 0), axis=2)
    n_used = jnp.sum(padded, axis=1) // g
    valid = jnp.arange(MOE_GRANULES, dtype=I32)[None, :] < n_used[:, None]
    gdest = jnp.where(valid, dest // g, 0).astype(I32).reshape(-1)
    n_tiles = _moe_tiles(n_tokens)
    tile_row = jnp.arange(n_tiles, dtype=I32) * MOE_ROW_TILE
    tile_expert = jnp.minimum(jnp.sum(((gbase + total_pad)[None, :] <= tile_row[:, None]).astype(I32), axis=1),
                              N_EXPERTS - 1).astype(I32)
    tiles_used = ((gbase[-1] + total_pad[-1]) // MOE_ROW_TILE).astype(I32).reshape(1)
    back = (jnp.arange(nc, dtype=I32)[:, None] * MOE_CHUNK_ROWS + loc_off - gbase[None, :] - coff) // g
    run_end = (gbase[None, :] + coff + padded) // g
    gran = jnp.arange(n_tiles * MOE_TILE_GRANULES, dtype=I32)
    own = (gbase[None, :] <= gran[:, None] * g) & (gran[:, None] * g < (gbase + total_pad)[None, :])
    pick = lambda tbl: jnp.sum(jnp.where(own[:, None, :], tbl[None, :, :], 0), axis=2)
    hops = jnp.sum(jnp.where(pick(run_end[:-1]) <= gran[:, None], pick(back[1:] - back[:-1]), 0), axis=1)
    src = gran + pick(back[:1])[:, 0] + hops
    filled = gran < pick(((gbase + total) // g)[None, :])[:, 0]
    src = jnp.where(filled, src, MOE_GRANULES - 1).astype(I32)
    return gdest, tile_expert, tiles_used, src


def _moe_tiles(n_tokens):
    rows = (n_tokens // MOE_CHUNK) * MOE_CHUNK_ROWS + N_EXPERTS * MOE_ROW_TILE
    return -(-rows // MOE_ROW_TILE)


def _granule_copy(src, dst, sem):
    return pltpu.make_async_copy(src, dst, sem)


def _moe_expert_kernel(src_ref, te_ref, nt_ref, xs_ref, wg_ref, wu_ref, wd_ref, o_ref, xbuf_ref, sem):
    del te_ref
    i = pl.program_id(0)
    n = nt_ref[0]
    g = MOE_GRANULE

    def copy(tile, k):
        s = src_ref[tile * MOE_TILE_GRANULES + k]
        return _granule_copy(xs_ref.at[pl.ds(pl.multiple_of(s * g, g), g)],
                             xbuf_ref.at[tile % 2, pl.ds(pl.multiple_of(k * g, g), g)], sem.at[tile % 2])

    def for_tile(tile, action):
        def step(k, carry):
            action(copy(tile, k))
            return carry
        lax.fori_loop(0, MOE_TILE_GRANULES, step, 0, unroll=MOE_DMA_UNROLL)

    @pl.when(i == 0)
    def _():
        for_tile(i, lambda cp: cp.start())

    @pl.when(i + 1 < n)
    def _():
        for_tile(i + 1, lambda cp: cp.start())

    @pl.when(i < n)
    def _():
        for_tile(i, lambda cp: cp.wait())
        x = xbuf_ref[i % 2]
        gate = _dot(x, wg_ref[...])
        up = _dot(x, wu_ref[...])
        act = (gate * jax.nn.sigmoid(gate) * up).astype(BF16)
        o_ref[...] = _dot(act, wd_ref[...]).astype(o_ref.dtype)

    @pl.when(i >= n)
    def _():
        o_ref[...] = jnp.zeros_like(o_ref)


def _moe_expert(src, tile_expert, tiles_used, xs, wg, wu, wd):
    nt = tile_expert.shape[0]

    def expert(i, src_ref, te_ref, nt_ref):
        return te_ref[jnp.minimum(i, nt_ref[0] - 1)]

    return pl.pallas_call(
        _moe_expert_kernel,
        grid_spec=pltpu.PrefetchScalarGridSpec(
            num_scalar_prefetch=3,
            grid=(nt,),
            in_specs=[pl.BlockSpec(memory_space=pl.ANY),
                      pl.BlockSpec((None, D_MODEL, EXPERT_DIM), lambda i, *p: (expert(i, *p), 0, 0)),
                      pl.BlockSpec((None, D_MODEL, EXPERT_DIM), lambda i, *p: (expert(i, *p), 0, 0)),
                      pl.BlockSpec((None, EXPERT_DIM, D_MODEL), lambda i, *p: (expert(i, *p), 0, 0))],
            out_specs=pl.BlockSpec((MOE_ROW_TILE, D_MODEL), lambda i, *p: (i, 0)),
            scratch_shapes=[pltpu.VMEM((2, MOE_ROW_TILE, D_MODEL), BF16), pltpu.SemaphoreType.DMA((2,))]),
        out_shape=jax.ShapeDtypeStruct((nt * MOE_ROW_TILE, D_MODEL), BF16),
        compiler_params=_params("arbitrary"),
        name="moe_expert",
    )(src, tile_expert, tiles_used, xs, wg, wu, wd)


def _moe_combine_kernel(gd_ref, h_ref, pos_ref, gate_ref, fg_ref, ys_ref, o_ref, ybuf_ref, sem,
                        *, final_norm):
    c = pl.program_id(0)
    nc = pl.num_programs(0)
    g = MOE_GRANULE
    t = MOE_CHUNK
    slot = c % 2

    def copy(chunk, j):
        d = gd_ref[chunk * MOE_GRANULES + j]
        return _granule_copy(ys_ref.at[pl.ds(pl.multiple_of(d * g, g), g)],
                             ybuf_ref.at[chunk % 2, pl.ds(pl.multiple_of(j * g, g), g)], sem.at[chunk % 2])

    def for_chunk(chunk, action):
        def step(j, carry):
            action(copy(chunk, j))
            return carry
        lax.fori_loop(0, MOE_GRANULES, step, 0, unroll=MOE_DMA_UNROLL)

    @pl.when(c == 0)
    def _():
        for_chunk(c, lambda cp: cp.start())

    @pl.when(c + 1 < nc)
    def _():
        for_chunk(c + 1, lambda cp: cp.start())

    for_chunk(c, lambda cp: cp.wait())

    rio = lax.broadcasted_iota(I32, (MOE_CHUNK_ROWS, t), 0)
    w = (jnp.where(rio == pos_ref[0:1, :], gate_ref[0:1, :], 0.0)
         + jnp.where(rio == pos_ref[1:2, :], gate_ref[1:2, :], 0.0)).astype(BF16)
    y = lax.dot_general(w, ybuf_ref[slot], (((0,), (0,)), ((), ())), preferred_element_type=F32)
    out = h_ref[...] + y
    if final_norm:
        out = _rms(out, fg_ref[...])
    o_ref[...] = out


def _moe_combine(gdest, h, pos, gates, ys, final_gain, final_norm):
    n = h.shape[0]
    t = MOE_CHUNK
    nc = n // t
    return pl.pallas_call(
        functools.partial(_moe_combine_kernel, final_norm=final_norm),
        grid_spec=pltpu.PrefetchScalarGridSpec(
            num_scalar_prefetch=1,
            grid=(nc,),
            in_specs=[pl.BlockSpec((t, D_MODEL), lambda i, gd: (i, 0)),
                      pl.BlockSpec((None, N_EXPERTS, t), lambda i, gd: (i, 0, 0)),
                      pl.BlockSpec((None, N_EXPERTS, t), lambda i, gd: (i, 0, 0)),
                      pl.BlockSpec(final_gain.shape, lambda i, gd: (0, 0)),
                      pl.BlockSpec(memory_space=pl.ANY)],
            out_specs=pl.BlockSpec((t, D_MODEL), lambda i, gd: (i, 0)),
            scratch_shapes=[pltpu.VMEM((2, MOE_CHUNK_ROWS, D_MODEL), BF16), pltpu.SemaphoreType.DMA((2,))]),
        out_shape=jax.ShapeDtypeStruct((n, D_MODEL), F32),
        compiler_params=_params("arbitrary"),
        name="moe_combine",
    )(gdest, h, pos, gates, final_gain, ys)


def _moe(a, wo, h, gain, router, wg, wu, wd, final_gain, final_norm):
    n = h.shape[0]
    wr = jnp.pad(router, ((0, 0), (0, LANES - N_EXPERTS)))
    wr_hi = wr.astype(BF16)
    wr_lo = (wr - wr_hi.astype(F32)).astype(BF16)
    h, xs, pos, gates, cnt = _moe_route(a, wo, h, gain, jnp.concatenate([wr_hi, wr_lo], axis=1))
    gdest, tile_expert, tiles_used, src = _moe_plan(cnt[:, :, 0], n)
    ys = _moe_expert(src, tile_expert, tiles_used, xs, wg.astype(BF16), wu.astype(BF16), wd.astype(BF16))
    return _moe_combine(gdest, h, pos, gates, ys, final_gain, final_norm)


def _rot_cols(w):
    half = MLA_ROPE // 2
    return jnp.concatenate([-w[..., half:], w[..., :half]], axis=-1)


def _rope_table(seq):
    inv = 1.0 / (ROPE_THETA ** (jnp.arange(0, MLA_ROPE, 2, dtype=F32) / MLA_ROPE))
    ang = jnp.arange(seq, dtype=F32)[:, None] * inv[None, :]
    cos, sin = jnp.cos(ang), jnp.sin(ang)
    return jnp.concatenate([cos, cos, sin, sin], axis=-1)


def _mla_weights(w_dqkv, w_uq, w_uk, w_uv):
    lat = MLA_Q_RANK + MLA_KV_RANK
    rope = w_dqkv[:, lat:]
    wd = jnp.concatenate([w_dqkv[:, :lat], rope, _rot_cols(rope)], axis=1).astype(BF16)
    q_rope = w_uq[..., MLA_NOPE:]
    wuq = jnp.concatenate([w_uq[..., :MLA_NOPE], q_rope, _rot_cols(q_rope)], axis=-1)
    wuq = wuq.reshape(MLA_Q_RANK, MLA_HEADS * MLA_HEAD_LANES).astype(BF16)
    wuk = w_uk.reshape(MLA_KV_RANK, -1).astype(BF16)
    wuvt = w_uv.reshape(MLA_KV_RANK, -1).T.astype(BF16)
    return wd, wuq, wuk, wuvt


def _swa_weights(w_qkv):
    qw = SWA_Q_HEADS * SWA_HEAD_DIM
    kvw = SWA_KV_HEADS * SWA_HEAD_DIM
    dup = lambda w: jnp.concatenate([w.reshape(D_MODEL, SWA_KV_HEADS, 1, SWA_HEAD_DIM)] * 2, axis=2).reshape(D_MODEL, -1)
    wq = w_qkv[:, :qw] * (SWA_HEAD_DIM ** -0.5 * LOG2_E)
    return jnp.concatenate([wq, dup(w_qkv[:, qw:qw + kvw]), dup(w_qkv[:, qw + kvw:])], axis=1).astype(BF16)


def kernel(x, mla_norm, mla_w_dqkv, mla_q_norm, mla_w_uq, mla_kv_norm, mla_w_uk, mla_w_uv, mla_w_o, swa_norm, swa_w_qkv, swa_sink, swa_w_o, ffn_norm, ffn_w_gate, ffn_w_up, ffn_w_down, moe_norm, moe_router, moe_w_gate, moe_w_up, moe_w_down, final_norm):
    batch, seq, _ = x.shape
    h = x.reshape(batch * seq, D_MODEL)
    cs = _rope_table(seq)
    swa_bias = _swa_bias_table()
    row = lambda v: v.reshape(1, -1).astype(F32)
    depth = 2 * mla_norm.shape[0]
    for layer in range(depth):
        j = layer // 2
        if layer % 2 == 0:
            wd, wuq, wuk, wuvt = _mla_weights(mla_w_dqkv[j], mla_w_uq[j], mla_w_uk[j], mla_w_uv[j])
            q, k, vt = _mla_proj(h, batch, seq, row(mla_norm[j]), wd, row(mla_q_norm[j]), row(mla_kv_norm[j]),
                                 wuq, wuk, wuvt, cs)
            a = _mla_attn(q, k, vt, batch, seq)
            h = _ffn(a, mla_w_o[j].astype(BF16), h, row(ffn_norm[j]), ffn_w_gate[j].astype(BF16),
                     ffn_w_up[j].astype(BF16), ffn_w_down[j].astype(BF16))
        else:
            qkv = _norm_matmul(h, row(swa_norm[j]), _swa_weights(swa_w_qkv[j]))
            a = _swa_attn(qkv, swa_sink[j].astype(F32), swa_bias, batch, seq)
            h = _moe(a, swa_w_o[j].astype(BF16), h, row(moe_norm[j]), moe_router[j], moe_w_gate[j], moe_w_up[j],
                     moe_w_down[j], row(final_norm), layer == depth - 1)
    return h.reshape(batch, seq, D_MODEL)
```

```python
import functools

import jax
import jax.numpy as jnp
from jax import lax
from jax.experimental import pallas as pl
from jax.experimental.pallas import tpu as pltpu

F32 = jnp.float32
BF16 = jnp.bfloat16
I32 = jnp.int32

D_MODEL = 1024
EPS = 1e-6
NEG_INF = -1e30
ROPE_THETA = 10000.0
LOG2_E = 1.4426950408889634

MLA_HEADS = 8
MLA_Q_RANK = 256
MLA_KV_RANK = 256
MLA_NOPE = 128
MLA_ROPE = 64
MLA_V = 128
MLA_QK = MLA_NOPE + MLA_ROPE
MLA_HEAD_LANES = 256

SWA_Q_HEADS = 16
SWA_KV_HEADS = 4
SWA_GROUP = 4
SWA_HEAD_DIM = 64
SWA_WINDOW = 128
SWA_BLOCK = 128
SWA_LOOKAHEAD = 2
SWA_STEP_BLOCKS = 8

FFN_DIM = 2816
N_EXPERTS = 8
EXPERT_DIM = 2048

LANES = 128
VMEM_LIMIT = 48 * 1024 * 1024

ROW_TILE = 512
ATTN_Q_TILE = 512
ATTN_STEP_HEADS = 4
ATTN_LOOKAHEAD = 2
MOE_CHUNK = 512
MOE_GRANULE = 16
MOE_CHUNK_ROWS = 2 * MOE_CHUNK + N_EXPERTS * MOE_GRANULE
MOE_GRANULES = MOE_CHUNK_ROWS // MOE_GRANULE
MOE_DMA_UNROLL = 8
MOE_ROW_TILE = 512
MOE_TILE_GRANULES = MOE_ROW_TILE // MOE_GRANULE


def _params(*sem, flags=None):
    return pltpu.CompilerParams(dimension_semantics=sem, vmem_limit_bytes=VMEM_LIMIT, flags=flags)


def _rms(x, g):
    return x * lax.rsqrt(jnp.mean(x * x, axis=-1, keepdims=True) + EPS) * g


def _dot(a, b):
    return jnp.dot(a, b, preferred_element_type=F32)


def _dot_nt(a, b):
    return lax.dot_general(a, b, (((1,), (1,)), ((), ())), preferred_element_type=F32)


def _mla_proj_kernel(h_ref, g_ref, wd_ref, qn_ref, kvn_ref, wuq_ref, wuk_ref, wuvt_ref, cs_ref,
                     q_ref, k_ref, vt_ref):
    xn = _rms(h_ref[...], g_ref[...]).astype(BF16)
    down = _dot(xn, wd_ref[...])
    cq = _rms(down[:, :MLA_Q_RANK], qn_ref[...]).astype(BF16)
    ckv = _rms(down[:, MLA_Q_RANK:MLA_Q_RANK + MLA_KV_RANK], kvn_ref[...]).astype(BF16)
    cs = cs_ref[...]
    lane = lax.broadcasted_iota(I32, cs.shape, 1)

    def rope(a):
        p = a * cs
        return p + pltpu.roll(p, MLA_ROPE, 1)

    kr = jnp.where(lane < MLA_ROPE, rope(down[:, 2 * MLA_Q_RANK:]), 0.0)
    q = _dot(cq, wuq_ref[...]) * (MLA_QK ** -0.5 * LOG2_E)
    for h in range(MLA_HEADS):
        c = h * MLA_HEAD_LANES
        q_ref[:, c:c + LANES] = q[:, c:c + LANES].astype(BF16)
        q_ref[:, c + LANES:c + 2 * LANES] = rope(q[:, c + LANES:c + 2 * LANES]).astype(BF16)
    nk = MLA_HEADS * MLA_NOPE
    k_ref[:, :nk] = _dot(ckv, wuk_ref[...]).astype(BF16)
    k_ref[:, nk:] = kr.astype(BF16)
    vt_ref[...] = _dot_nt(wuvt_ref[...], ckv).astype(BF16)


def _mla_proj(h, batch, seq, gain, wd, qn, kvn, wuq, wuk, wuvt, cs):
    n = h.shape[0]
    t = ROW_TILE
    sb = seq // t
    full = lambda a: pl.BlockSpec(a.shape, lambda i: (0, 0))
    return pl.pallas_call(
        _mla_proj_kernel,
        grid=(n // t,),
        in_specs=[pl.BlockSpec((t, D_MODEL), lambda i: (i, 0)), full(gain), full(wd), full(qn), full(kvn),
                  full(wuq), full(wuk), full(wuvt), pl.BlockSpec((t, LANES), lambda i: (i % sb, 0))],
        out_specs=[pl.BlockSpec((t, MLA_HEADS * MLA_HEAD_LANES), lambda i: (i, 0)),
                   pl.BlockSpec((t, MLA_HEADS * MLA_NOPE + LANES), lambda i: (i, 0)),
                   pl.BlockSpec((None, MLA_HEADS * MLA_V, t), lambda i: (i // sb, 0, i % sb))],
        out_shape=[jax.ShapeDtypeStruct((n, MLA_HEADS * MLA_HEAD_LANES), BF16),
                   jax.ShapeDtypeStruct((n, MLA_HEADS * MLA_NOPE + LANES), BF16),
                   jax.ShapeDtypeStruct((batch, MLA_HEADS * MLA_V, seq), BF16)],
        compiler_params=_params("parallel"),
        name="mla_proj",
    )(h, gain, wd, qn, kvn, wuq, wuk, wuvt, cs)


def _mla_attn_kernel(q_ref, kn_ref, kr_ref, vt_ref, o_ref, kcat_ref):
    hp = ATTN_STEP_HEADS
    for h in range(hp):
        kcat_ref[h, :, :LANES] = kn_ref[:, h * LANES:(h + 1) * LANES]
        kcat_ref[h, :, LANES:] = kr_ref[...]
    seq = q_ref.shape[0]
    tq = ATTN_Q_TILE
    work = [(h, t) for h in range(hp) for t in range(seq // tq)]

    def scores(h, t):
        return _dot_nt(kcat_ref[h], q_ref[t * tq:(t + 1) * tq, h * MLA_HEAD_LANES:(h + 1) * MLA_HEAD_LANES])

    def attend(h, st):
        pt = jnp.exp2(st - jnp.max(st, axis=0, keepdims=True))
        l = jnp.sum(pt, axis=0, keepdims=True)
        ot = _dot(vt_ref[h * MLA_V:(h + 1) * MLA_V, :], pt.astype(BF16)) / l
        return ot.T.astype(o_ref.dtype)

    ahead = ATTN_LOOKAHEAD
    pending = [scores(*w) for w in work[:ahead]]
    outs = {}
    for n, (h, t) in enumerate(work):
        if n + ahead < len(work):
            pending.append(scores(*work[n + ahead]))
        outs[h, t] = attend(h, pending.pop(0))
    o_ref[...] = jnp.concatenate(
        [jnp.concatenate([outs[h, t] for t in range(seq // tq)], axis=0) for h in range(hp)], axis=1)


def _mla_attn(q, k, vt, batch, seq):
    n = q.shape[0]
    hp = ATTN_STEP_HEADS
    return pl.pallas_call(
        _mla_attn_kernel,
        grid=(batch, MLA_HEADS // hp),
        in_specs=[pl.BlockSpec((seq, hp * MLA_HEAD_LANES), lambda b, h: (b, h)),
                  pl.BlockSpec((seq, hp * LANES), lambda b, h: (b, h)),
                  pl.BlockSpec((seq, LANES), lambda b, h: (b, MLA_HEADS)),
                  pl.BlockSpec((None, hp * MLA_V, seq), lambda b, h: (b, h, 0))],
        out_specs=pl.BlockSpec((seq, hp * MLA_V), lambda b, h: (b, h)),
        out_shape=jax.ShapeDtypeStruct((n, MLA_HEADS * MLA_V), BF16),
        scratch_shapes=[pltpu.VMEM((hp, seq, MLA_HEAD_LANES), BF16)],
        compiler_params=_params("parallel", "arbitrary"),
        name="mla_attn",
    )(q, k, k, vt)


def _ffn_kernel(a_ref, wo_ref, h_ref, g_ref, wg_ref, wu_ref, wd_ref, o_ref):
    h1 = h_ref[...] + _dot(a_ref[...], wo_ref[...])
    xn = _rms(h1, g_ref[...]).astype(BF16)
    gate = _dot(xn, wg_ref[...])
    up = _dot(xn, wu_ref[...])
    act = (gate * jax.nn.sigmoid(gate) * up).astype(BF16)
    o_ref[...] = h1 + _dot(act, wd_ref[...])


def _resident(a):
    return pl.BlockSpec(a.shape, lambda i: (0,) * a.ndim, pipeline_mode=pl.Buffered(1))


def _ffn(a, wo, h, gain, wg, wu, wd):
    n = h.shape[0]
    t = ROW_TILE
    return pl.pallas_call(
        _ffn_kernel,
        grid=(n // t,),
        in_specs=[pl.BlockSpec((t, a.shape[1]), lambda i: (i, 0)), _resident(wo),
                  pl.BlockSpec((t, D_MODEL), lambda i: (i, 0)), _resident(gain),
                  _resident(wg), _resident(wu), _resident(wd)],
        out_specs=pl.BlockSpec((t, D_MODEL), lambda i: (i, 0)),
        out_shape=jax.ShapeDtypeStruct((n, D_MODEL), F32),
        compiler_params=_params("parallel"),
        name="ffn",
    )(a, wo, h, gain, wg, wu, wd)


def _norm_matmul_kernel(h_ref, g_ref, w_ref, o_ref):
    o_ref[...] = _dot(_rms(h_ref[...], g_ref[...]).astype(BF16), w_ref[...]).astype(o_ref.dtype)


def _norm_matmul(h, gain, w):
    n = h.shape[0]
    t = ROW_TILE
    return pl.pallas_call(
        _norm_matmul_kernel,
        grid=(n // t,),
        in_specs=[pl.BlockSpec((t, D_MODEL), lambda i: (i, 0)),
                  pl.BlockSpec(gain.shape, lambda i: (0, 0)),
                  pl.BlockSpec(w.shape, lambda i: (0, 0))],
        out_specs=pl.BlockSpec((t, w.shape[1]), lambda i: (i, 0)),
        out_shape=jax.ShapeDtypeStruct((n, w.shape[1]), BF16),
        compiler_params=_params("parallel"),
        name="norm_matmul",
    )(h, gain, w)


def _alibi_slope(head):
    return 2.0 ** (-8.0 * (head + 1) / SWA_Q_HEADS)


def _swa_bias_table():
    t = jnp.arange(SWA_BLOCK, dtype=I32)[:, None]
    s = jnp.arange(SWA_BLOCK, dtype=I32)[None, :]
    slots = []
    for j in (-1, 0, 1):
        dist = jnp.abs(t - (s + j * SWA_BLOCK))
        slots.append(jnp.where(dist <= SWA_WINDOW, -dist.astype(F32), NEG_INF))
    slots.append(jnp.full((SWA_BLOCK, SWA_BLOCK), NEG_INF, F32))
    base = jnp.stack(slots)
    slopes = jnp.asarray([_alibi_slope(h) * LOG2_E for h in range(SWA_Q_HEADS)], F32)
    per_head = slopes[:, None, None, None] * base[None]
    return jnp.concatenate([per_head[0::2], per_head[1::2]], axis=-1)


def _swa_attn_kernel(sink_ref, q_ref, k_ref, v_ref, bias_ref, o_ref, klo_ref, khi_ref, vlo_ref, vhi_ref):
    blk = SWA_BLOCK
    nb = k_ref.shape[0] // blk
    lane = lax.broadcasted_iota(I32, (blk, LANES), 1)
    lo = lane < SWA_HEAD_DIM

    @pl.when(pl.program_id(1) == 0)
    def _():
        keep = lax.broadcasted_iota(I32, k_ref.shape, 1) % LANES < SWA_HEAD_DIM
        klo_ref[...] = jnp.where(keep, k_ref[...], 0)
        khi_ref[...] = jnp.where(keep, 0, k_ref[...])
        vlo_ref[...] = jnp.where(keep, v_ref[...], 0)
        vhi_ref[...] = jnp.where(keep, 0, v_ref[...])

    def block(jb, carry):
        i = pl.program_id(1) * SWA_STEP_BLOCKS + jb
        q0 = pl.multiple_of(jb * blk, blk)
        rows, slots = [], []
        for j in (-1, 0, 1):
            kb = i + j
            rows.append(pl.multiple_of(jnp.clip(kb, 0, nb - 1) * blk, blk))
            slots.append(jnp.where((kb >= 0) & (kb < nb), j + 1, 3))

        def split(lo_ref, hi_ref, g, r0):
            return jnp.concatenate([lo_ref[pl.ds(r0, blk), g * LANES:(g + 1) * LANES],
                                    hi_ref[pl.ds(r0, blk), g * LANES:(g + 1) * LANES]], axis=0)

        def scores(pair):
            g = 2 * pair // SWA_GROUP
            qp = q_ref[pl.ds(q0, blk), pair * LANES:(pair + 1) * LANES]
            return [_dot_nt(qp, split(klo_ref, khi_ref, g, rows[j])) + bias_ref[pair, slots[j]] for j in range(3)]

        def attend(pair, ss):
            h0 = 2 * pair
            ps, ls = [], []
            for half, head in ((0, h0), (1, h0 + 1)):
                sl = slice(half * blk, (half + 1) * blk)
                snk = sink_ref[head] * LOG2_E
                m = jnp.max(jnp.maximum(jnp.maximum(ss[0][:, sl], ss[1][:, sl]), ss[2][:, sl]),
                            axis=-1, keepdims=True)
                m = jnp.maximum(m, snk)
                p3 = [jnp.exp2(ss[j][:, sl] - m) for j in range(3)]
                ls.append(jnp.sum(p3[0] + p3[1] + p3[2], axis=-1, keepdims=True) + jnp.exp2(snk - m))
                ps.append(p3)
            o = jnp.zeros((blk, LANES), F32)
            for j in range(3):
                pj = jnp.concatenate([ps[0][j], ps[1][j]], axis=1).astype(BF16)
                o = o + _dot(pj, split(vlo_ref, vhi_ref, h0 // SWA_GROUP, rows[j]))
            return (o / jnp.where(lo, ls[0], ls[1])).astype(o_ref.dtype)

        n_pairs = SWA_Q_HEADS // 2
        pending = {p: scores(p) for p in range(min(SWA_LOOKAHEAD, n_pairs))}
        outs = []
        for p in range(n_pairs):
            if p + SWA_LOOKAHEAD < n_pairs:
                pending[p + SWA_LOOKAHEAD] = scores(p + SWA_LOOKAHEAD)
            outs.append(attend(p, pending.pop(p)))
        o_ref[pl.ds(q0, blk), :] = jnp.concatenate(outs, axis=1)
        return carry

    lax.fori_loop(0, SWA_STEP_BLOCKS, block, 0)


def _swa_attn(qkv, sink, bias, batch, seq):
    n = qkv.shape[0]
    rows = SWA_STEP_BLOCKS * SWA_BLOCK
    steps = seq // rows
    qw = SWA_Q_HEADS * SWA_HEAD_DIM
    kvw = SWA_KV_HEADS * LANES
    kblk = qw // kvw
    return pl.pallas_call(
        _swa_attn_kernel,
        grid=(batch, steps),
        in_specs=[pl.BlockSpec(memory_space=pltpu.SMEM),
                  pl.BlockSpec((rows, qw), lambda b, i: (b * steps + i, 0)),
                  pl.BlockSpec((seq, kvw), lambda b, i: (b, kblk)),
                  pl.BlockSpec((seq, kvw), lambda b, i: (b, kblk + 1)),
                  pl.BlockSpec(bias.shape, lambda b, i: (0, 0, 0, 0), pipeline_mode=pl.Buffered(1))],
        out_specs=pl.BlockSpec((rows, qw), lambda b, i: (b * steps + i, 0)),
        out_shape=jax.ShapeDtypeStruct((n, qw), BF16),
        scratch_shapes=[pltpu.VMEM((seq, kvw), BF16) for _ in range(4)],
        compiler_params=_params("arbitrary", "arbitrary"),
        name="swa_attn",
    )(sink, qkv, qkv, qkv, bias)


def _moe_route_kernel(a_ref, wo_ref, h_ref, g_ref, wr_ref, h1_ref, xs_ref, pos_ref, gate_ref, cnt_ref):
    t = MOE_CHUNK
    h1 = h_ref[...] + _dot(a_ref[...], wo_ref[...])
    h1_ref[...] = h1
    xn = _rms(h1, g_ref[...])
    x_hi = xn.astype(BF16)
    x_lo = (xn - x_hi.astype(F32)).astype(BF16)
    hi = _dot(x_hi, wr_ref[...])
    logits = hi[:, :LANES] + hi[:, LANES:] + _dot(x_lo, wr_ref[:, :LANES])
    lt = logits.T[:N_EXPERTS]
    eio = lax.broadcasted_iota(I32, lt.shape, 0)
    m1 = jnp.max(lt, axis=0, keepdims=True)
    i1 = jnp.min(jnp.where(lt == m1, eio, N_EXPERTS), axis=0, keepdims=True)
    l2 = jnp.where(eio == i1, -jnp.inf, lt)
    m2 = jnp.max(l2, axis=0, keepdims=True)
    i2 = jnp.min(jnp.where(l2 == m2, eio, N_EXPERTS), axis=0, keepdims=True)
    e21 = jnp.exp(m2 - m1)
    g1 = 1.0 / (1.0 + e21)
    g2 = e21 * g1

    sel1 = eio == i1
    sel2 = eio == i2
    oh = jnp.where(sel1 | sel2, 1.0, 0.0)
    tr = lax.broadcasted_iota(I32, (t, t), 0)
    tc = lax.broadcasted_iota(I32, (t, t), 1)
    before = jnp.where(tr < tc, 1.0, 0.0).astype(BF16)
    rank = _dot(oh.astype(BF16), before)
    cnt = jnp.sum(oh, axis=1, keepdims=True).astype(I32)
    padded = jnp.bitwise_and(cnt + (MOE_GRANULE - 1), -MOE_GRANULE)
    offs, run = [], jnp.zeros((1, 1), I32)
    for e in range(N_EXPERTS):
        offs.append(run)
        run = run + padded[e:e + 1, :]
    off = jnp.concatenate(offs, axis=0).astype(F32)
    slot = rank + off
    pos1 = jnp.sum(jnp.where(sel1, slot, 0.0), axis=0, keepdims=True).astype(I32)
    pos2 = jnp.sum(jnp.where(sel2, slot, 0.0), axis=0, keepdims=True).astype(I32)

    rio = lax.broadcasted_iota(I32, (MOE_CHUNK_ROWS, t), 0)
    perm = jnp.where(rio == pos1, 1.0, jnp.where(rio == pos2, 1.0, 0.0)).astype(BF16)
    xs_ref[...] = _dot(perm, x_hi).astype(BF16)

    zi = jnp.zeros((N_EXPERTS - 2, t), I32)
    pos_ref[...] = jnp.concatenate([pos1, pos2, zi], axis=0)
    gate_ref[...] = jnp.concatenate([g1, g2, zi.astype(F32)], axis=0)
    cnt_ref[...] = jnp.broadcast_to(padded, (N_EXPERTS, LANES))


def _moe_route(a, wo, h, gain, wr):
    n = h.shape[0]
    t = MOE_CHUNK
    nc = n // t
    return pl.pallas_call(
        _moe_route_kernel,
        grid=(nc,),
        in_specs=[pl.BlockSpec((t, a.shape[1]), lambda i: (i, 0)),
                  pl.BlockSpec(wo.shape, lambda i: (0, 0)),
                  pl.BlockSpec((t, D_MODEL), lambda i: (i, 0)),
                  pl.BlockSpec(gain.shape, lambda i: (0, 0)),
                  pl.BlockSpec(wr.shape, lambda i: (0, 0))],
        out_specs=[pl.BlockSpec((t, D_MODEL), lambda i: (i, 0)),
                   pl.BlockSpec((MOE_CHUNK_ROWS, D_MODEL), lambda i: (i, 0)),
                   pl.BlockSpec((None, N_EXPERTS, t), lambda i: (i, 0, 0)),
                   pl.BlockSpec((None, N_EXPERTS, t), lambda i: (i, 0, 0)),
                   pl.BlockSpec((None, N_EXPERTS, LANES), lambda i: (i, 0, 0))],
        out_shape=[jax.ShapeDtypeStruct((n, D_MODEL), F32),
                   jax.ShapeDtypeStruct((nc * MOE_CHUNK_ROWS, D_MODEL), BF16),
                   jax.ShapeDtypeStruct((nc, N_EXPERTS, t), I32),
                   jax.ShapeDtypeStruct((nc, N_EXPERTS, t), F32),
                   jax.ShapeDtypeStruct((nc, N_EXPERTS, LANES), I32)],
        compiler_params=_params("parallel"),
        name="moe_route",
    )(a, wo, h, gain, wr)


def _moe_plan(padded, n_tokens):
    nc = padded.shape[0]
    g = MOE_GRANULE
    loc_off = jnp.cumsum(padded, axis=1) - padded
    total = jnp.sum(padded, axis=0)
    total_pad = ((total + MOE_ROW_TILE - 1) // MOE_ROW_TILE) * MOE_ROW_TILE
    gbase = jnp.cumsum(total_pad) - total_pad
    coff = jnp.cumsum(padded, axis=0) - padded
    shift = gbase[None, :] + coff - loc_off
    step = shift[:, 1:] - shift[:, :-1]
    ends = (loc_off + padded)[:, :-1]
    row = jnp.arange(MOE_GRANULES, dtype=I32)[None, :, None] * g
    dest = row[:, :, 0] + shift[:, :1] + jnp.sum(jnp.where(ends[:, None, :] <= row, step[:, None, :], 0), axis=2)
    n_used = jnp.sum(padded, axis=1) // g
    valid = jnp.arange(MOE_GRANULES, dtype=I32)[None, :] < n_used[:, None]
    gdest = jnp.where(valid, dest // g, 0).astype(I32).reshape(-1)
    n_tiles = _moe_tiles(n_tokens)
    tile_row = jnp.arange(n_tiles, dtype=I32) * MOE_ROW_TILE
    tile_expert = jnp.minimum(jnp.sum(((gbase + total_pad)[None, :] <= tile_row[:, None]).astype(I32), axis=1),
                              N_EXPERTS - 1).astype(I32)
    tiles_used = ((gbase[-1] + total_pad[-1]) // MOE_ROW_TILE).astype(I32).reshape(1)
    back = (jnp.arange(nc, dtype=I32)[:, None] * MOE_CHUNK_ROWS + loc_off - gbase[None, :] - coff) // g
    run_end = (gbase[None, :] + coff + padded) // g
    gran = jnp.arange(n_tiles * MOE_TILE_GRANULES, dtype=I32)
    own = (gbase[None, :] <= gran[:, None] * g) & (gran[:, None] * g < (gbase + total_pad)[None, :])
    pick = lambda tbl: jnp.sum(jnp.where(own[:, None, :], tbl[None, :, :], 0), axis=2)
    hops = jnp.sum(jnp.where(pick(run_end[:-1]) <= gran[:, None], pick(back[1:] - back[:-1]), 0), axis=1)
    src = gran + pick(back[:1])[:, 0] + hops
    filled = gran < pick(((gbase + total) // g)[None, :])[:, 0]
    src = jnp.where(filled, src, MOE_GRANULES - 1).astype(I32)
    return gdest, tile_expert, tiles_used, src


def _moe_tiles(n_tokens):
    rows = (n_tokens // MOE_CHUNK) * MOE_CHUNK_ROWS + N_EXPERTS * MOE_ROW_TILE
    return -(-rows // MOE_ROW_TILE)


def _granule_copy(src, dst, sem):
    return pltpu.make_async_copy(src, dst, sem)


def _moe_expert_kernel(src_ref, te_ref, nt_ref, xs_ref, wg_ref, wu_ref, wd_ref, o_ref, xbuf_ref, sem):
    del te_ref
    i = pl.program_id(0)
    n = nt_ref[0]
    g = MOE_GRANULE

    def copy(tile, k):
        s = src_ref[tile * MOE_TILE_GRANULES + k]
        return _granule_copy(xs_ref.at[pl.ds(pl.multiple_of(s * g, g), g)],
                             xbuf_ref.at[tile % 2, pl.ds(pl.multiple_of(k * g, g), g)], sem.at[tile % 2])

    def for_tile(tile, action):
        def step(k, carry):
            action(copy(tile, k))
            return carry
        lax.fori_loop(0, MOE_TILE_GRANULES, step, 0, unroll=MOE_DMA_UNROLL)

    @pl.when(i == 0)
    def _():
        for_tile(i, lambda cp: cp.start())

    @pl.when(i + 1 < n)
    def _():
        for_tile(i + 1, lambda cp: cp.start())

    @pl.when(i < n)
    def _():
        for_tile(i, lambda cp: cp.wait())
        x = xbuf_ref[i % 2]
        gate = _dot(x, wg_ref[...])
        up = _dot(x, wu_ref[...])
        act = (gate * jax.nn.sigmoid(gate) * up).astype(BF16)
        o_ref[...] = _dot(act, wd_ref[...]).astype(o_ref.dtype)

    @pl.when(i >= n)
    def _():
        o_ref[...] = jnp.zeros_like(o_ref)


def _moe_expert(src, tile_expert, tiles_used, xs, wg, wu, wd):
    nt = tile_expert.shape[0]

    def expert(i, src_ref, te_ref, nt_ref):
        return te_ref[jnp.minimum(i, nt_ref[0] - 1)]

    return pl.pallas_call(
        _moe_expert_kernel,
        grid_spec=pltpu.PrefetchScalarGridSpec(
            num_scalar_prefetch=3,
            grid=(nt,),
            in_specs=[pl.BlockSpec(memory_space=pl.ANY),
                      pl.BlockSpec((None, D_MODEL, EXPERT_DIM), lambda i, *p: (expert(i, *p), 0, 0)),
                      pl.BlockSpec((None, D_MODEL, EXPERT_DIM), lambda i, *p: (expert(i, *p), 0, 0)),
                      pl.BlockSpec((None, EXPERT_DIM, D_MODEL), lambda i, *p: (expert(i, *p), 0, 0))],
            out_specs=pl.BlockSpec((MOE_ROW_TILE, D_MODEL), lambda i, *p: (i, 0)),
            scratch_shapes=[pltpu.VMEM((2, MOE_ROW_TILE, D_MODEL), BF16), pltpu.SemaphoreType.DMA((2,))]),
        out_shape=jax.ShapeDtypeStruct((nt * MOE_ROW_TILE, D_MODEL), BF16),
        compiler_params=_params("arbitrary"),
        name="moe_expert",
    )(src, tile_expert, tiles_used, xs, wg, wu, wd)


def _moe_combine_kernel(gd_ref, h_ref, pos_ref, gate_ref, fg_ref, ys_ref, o_ref, ybuf_ref, sem,
                        *, final_norm):
    c = pl.program_id(0)
    nc = pl.num_programs(0)
    g = MOE_GRANULE
    t = MOE_CHUNK
    slot = c % 2

    def copy(chunk, j):
        d = gd_ref[chunk * MOE_GRANULES + j]
        return _granule_copy(ys_ref.at[pl.ds(pl.multiple_of(d * g, g), g)],
                             ybuf_ref.at[chunk % 2, pl.ds(pl.multiple_of(j * g, g), g)], sem.at[chunk % 2])

    def for_chunk(chunk, action):
        def step(j, carry):
            action(copy(chunk, j))
            return carry
        lax.fori_loop(0, MOE_GRANULES, step, 0, unroll=MOE_DMA_UNROLL)

    @pl.when(c == 0)
    def _():
        for_chunk(c, lambda cp: cp.start())

    @pl.when(c + 1 < nc)
    def _():
        for_chunk(c + 1, lambda cp: cp.start())

    for_chunk(c, lambda cp: cp.wait())

    rio = lax.broadcasted_iota(I32, (MOE_CHUNK_ROWS, t), 0)
    w = (jnp.where(rio == pos_ref[0:1, :], gate_ref[0:1, :], 0.0)
         + jnp.where(rio == pos_ref[1:2, :], gate_ref[1:2, :], 0.0)).astype(BF16)
    y = lax.dot_general(w, ybuf_ref[slot], (((0,), (0,)), ((), ())), preferred_element_type=F32)
    out = h_ref[...] + y
    if final_norm:
        out = _rms(out, fg_ref[...])
    o_ref[...] = out


def _moe_combine(gdest, h, pos, gates, ys, final_gain, final_norm):
    n = h.shape[0]
    t = MOE_CHUNK
    nc = n // t
    return pl.pallas_call(
        functools.partial(_moe_combine_kernel, final_norm=final_norm),
        grid_spec=pltpu.PrefetchScalarGridSpec(
            num_scalar_prefetch=1,
            grid=(nc,),
            in_specs=[pl.BlockSpec((t, D_MODEL), lambda i, gd: (i, 0)),
                      pl.BlockSpec((None, N_EXPERTS, t), lambda i, gd: (i, 0, 0)),
                      pl.BlockSpec((None, N_EXPERTS, t), lambda i, gd: (i, 0, 0)),
                      pl.BlockSpec(final_gain.shape, lambda i, gd: (0, 0)),
                      pl.BlockSpec(memory_space=pl.ANY)],
            out_specs=pl.BlockSpec((t, D_MODEL), lambda i, gd: (i, 0)),
            scratch_shapes=[pltpu.VMEM((2, MOE_CHUNK_ROWS, D_MODEL), BF16), pltpu.SemaphoreType.DMA((2,))]),
        out_shape=jax.ShapeDtypeStruct((n, D_MODEL), F32),
        compiler_params=_params("arbitrary"),
        name="moe_combine",
    )(gdest, h, pos, gates, final_gain, ys)


def _moe(a, wo, h, gain, router, wg, wu, wd, final_gain, final_norm):
    n = h.shape[0]
    wr = jnp.pad(router, ((0, 0), (0, LANES - N_EXPERTS)))
    wr_hi = wr.astype(BF16)
    wr_lo = (wr - wr_hi.astype(F32)).astype(BF16)
    h, xs, pos, gates, cnt = _moe_route(a, wo, h, gain, jnp.concatenate([wr_hi, wr_lo], axis=1))
    gdest, tile_expert, tiles_used, src = _moe_plan(cnt[:, :, 0], n)
    ys = _moe_expert(src, tile_expert, tiles_used, xs, wg.astype(BF16), wu.astype(BF16), wd.astype(BF16))
    return _moe_combine(gdest, h, pos, gates, ys, final_gain, final_norm)


def _rot_cols(w):
    half = MLA_ROPE // 2
    return jnp.concatenate([-w[..., half:], w[..., :half]], axis=-1)


def _rope_table(seq):
    inv = 1.0 / (ROPE_THETA ** (jnp.arange(0, MLA_ROPE, 2, dtype=F32) / MLA_ROPE))
    ang = jnp.arange(seq, dtype=F32)[:, None] * inv[None, :]
    cos, sin = jnp.cos(ang), jnp.sin(ang)
    return jnp.concatenate([cos, cos, sin, sin], axis=-1)


def _mla_weights(w_dqkv, w_uq, w_uk, w_uv):
    lat = MLA_Q_RANK + MLA_KV_RANK
    rope = w_dqkv[:, lat:]
    wd = jnp.concatenate([w_dqkv[:, :lat], rope, _rot_cols(rope)], axis=1).astype(BF16)
    q_rope = w_uq[..., MLA_NOPE:]
    wuq = jnp.concatenate([w_uq[..., :MLA_NOPE], q_rope, _rot_cols(q_rope)], axis=-1)
    wuq = wuq.reshape(MLA_Q_RANK, MLA_HEADS * MLA_HEAD_LANES).astype(BF16)
    wuk = w_uk.reshape(MLA_KV_RANK, -1).astype(BF16)
    wuvt = w_uv.reshape(MLA_KV_RANK, -1).T.astype(BF16)
    return wd, wuq, wuk, wuvt


def _swa_weights(w_qkv):
    qw = SWA_Q_HEADS * SWA_HEAD_DIM
    kvw = SWA_KV_HEADS * SWA_HEAD_DIM
    dup = lambda w: jnp.concatenate([w.reshape(D_MODEL, SWA_KV_HEADS, 1, SWA_HEAD_DIM)] * 2, axis=2).reshape(D_MODEL, -1)
    wq = w_qkv[:, :qw] * (SWA_HEAD_DIM ** -0.5 * LOG2_E)
    return jnp.concatenate([wq, dup(w_qkv[:, qw:qw + kvw]), dup(w_qkv[:, qw + kvw:])], axis=1).astype(BF16)


def kernel(x, mla_norm, mla_w_dqkv, mla_q_norm, mla_w_uq, mla_kv_norm, mla_w_uk, mla_w_uv, mla_w_o, swa_norm, swa_w_qkv, swa_sink, swa_w_o, ffn_norm, ffn_w_gate, ffn_w_up, ffn_w_down, moe_norm, moe_router, moe_w_gate, moe_w_up, moe_w_down, final_norm):
    batch, seq, _ = x.shape
    h = x.reshape(batch * seq, D_MODEL)
    cs = _rope_table(seq)
    swa_bias = _swa_bias_table()
    row = lambda v: v.reshape(1, -1).astype(F32)
    depth = 2 * mla_norm.shape[0]
    for layer in range(depth):
        j = layer // 2
        if layer % 2 == 0:
            wd, wuq, wuk, wuvt = _mla_weights(mla_w_dqkv[j], mla_w_uq[j], mla_w_uk[j], mla_w_uv[j])
            q, k, vt = _mla_proj(h, batch, seq, row(mla_norm[j]), wd, row(mla_q_norm[j]), row(mla_kv_norm[j]),
                                 wuq, wuk, wuvt, cs)
            a = _mla_attn(q, k, vt, batch, seq)
            h = _ffn(a, mla_w_o[j].astype(BF16), h, row(ffn_norm[j]), ffn_w_gate[j].astype(BF16),
                     ffn_w_up[j].astype(BF16), ffn_w_down[j].astype(BF16))
        else:
            qkv = _norm_matmul(h, row(swa_norm[j]), _swa_weights(swa_w_qkv[j]))
            a = _swa_attn(qkv, swa_sink[j].astype(F32), swa_bias, batch, seq)
            h = _moe(a, swa_w_o[j].astype(BF16), h, row(moe_norm[j]), moe_router[j], moe_w_gate[j], moe_w_up[j],
                     moe_w_down[j], row(final_norm), layer == depth - 1)
    return h.reshape(batch, seq, D_MODEL)
```

```python
import functools

import jax
import jax.numpy as jnp
from jax import lax
from jax.experimental import pallas as pl
from jax.experimental.pallas import tpu as pltpu

F32 = jnp.float32
BF16 = jnp.bfloat16
I32 = jnp.int32

D_MODEL = 1024
EPS = 1e-6
NEG_INF = -1e30
ROPE_THETA = 10000.0
LOG2_E = 1.4426950408889634

MLA_HEADS = 8
MLA_Q_RANK = 256
MLA_KV_RANK = 256
MLA_NOPE = 128
MLA_ROPE = 64
MLA_V = 128
MLA_QK = MLA_NOPE + MLA_ROPE
MLA_HEAD_LANES = 256

SWA_Q_HEADS = 16
SWA_KV_HEADS = 4
SWA_GROUP = 4
SWA_HEAD_DIM = 64
SWA_WINDOW = 128
SWA_BLOCK = 128
SWA_LOOKAHEAD = 2
SWA_STEP_BLOCKS = 8

FFN_DIM = 2816
N_EXPERTS = 8
EXPERT_DIM = 2048

LANES = 128
VMEM_LIMIT = 48 * 1024 * 1024

ROW_TILE = 512
ATTN_Q_TILE = 512
ATTN_STEP_HEADS = 4
ATTN_LOOKAHEAD = 2
MOE_CHUNK = 512
MOE_GRANULE = 16
MOE_CHUNK_ROWS = 2 * MOE_CHUNK + N_EXPERTS * MOE_GRANULE
MOE_GRANULES = MOE_CHUNK_ROWS // MOE_GRANULE
MOE_DMA_UNROLL = 8
MOE_ROW_TILE = 512
MOE_TILE_GRANULES = MOE_ROW_TILE // MOE_GRANULE


def _params(*sem, flags=None):
    return pltpu.CompilerParams(dimension_semantics=sem, vmem_limit_bytes=VMEM_LIMIT, flags=flags)


def _rms(x, g):
    return x * lax.rsqrt(jnp.mean(x * x, axis=-1, keepdims=True) + EPS) * g


def _dot(a, b):
    return jnp.dot(a, b, preferred_element_type=F32)


def _dot_nt(a, b):
    return lax.dot_general(a, b, (((1,), (1,)), ((), ())), preferred_element_type=F32)


def _mla_proj_kernel(h_ref, g_ref, wd_ref, qn_ref, kvn_ref, wuq_ref, wuk_ref, wuvt_ref, cs_ref,
                     q_ref, k_ref, vt_ref):
    xn = _rms(h_ref[...], g_ref[...]).astype(BF16)
    down = _dot(xn, wd_ref[...])
    cq = _rms(down[:, :MLA_Q_RANK], qn_ref[...]).astype(BF16)
    ckv = _rms(down[:, MLA_Q_RANK:MLA_Q_RANK + MLA_KV_RANK], kvn_ref[...]).astype(BF16)
    cs = cs_ref[...]
    lane = lax.broadcasted_iota(I32, cs.shape, 1)

    def rope(a):
        p = a * cs
        return p + pltpu.roll(p, MLA_ROPE, 1)

    kr = jnp.where(lane < MLA_ROPE, rope(down[:, 2 * MLA_Q_RANK:]), 0.0)
    q = _dot(cq, wuq_ref[...]) * (MLA_QK ** -0.5 * LOG2_E)
    for h in range(MLA_HEADS):
        c = h * MLA_HEAD_LANES
        q_ref[:, c:c + LANES] = q[:, c:c + LANES].astype(BF16)
        q_ref[:, c + LANES:c + 2 * LANES] = rope(q[:, c + LANES:c + 2 * LANES]).astype(BF16)
    nk = MLA_HEADS * MLA_NOPE
    k_ref[:, :nk] = _dot(ckv, wuk_ref[...]).astype(BF16)
    k_ref[:, nk:] = kr.astype(BF16)
    vt_ref[...] = _dot_nt(wuvt_ref[...], ckv).astype(BF16)


def _mla_proj(h, batch, seq, gain, wd, qn, kvn, wuq, wuk, wuvt, cs):
    n = h.shape[0]
    t = ROW_TILE
    sb = seq // t
    full = lambda a: pl.BlockSpec(a.shape, lambda i: (0, 0))
    return pl.pallas_call(
        _mla_proj_kernel,
        grid=(n // t,),
        in_specs=[pl.BlockSpec((t, D_MODEL), lambda i: (i, 0)), full(gain), full(wd), full(qn), full(kvn),
                  full(wuq), full(wuk), full(wuvt), pl.BlockSpec((t, LANES), lambda i: (i % sb, 0))],
        out_specs=[pl.BlockSpec((t, MLA_HEADS * MLA_HEAD_LANES), lambda i: (i, 0)),
                   pl.BlockSpec((t, MLA_HEADS * MLA_NOPE + LANES), lambda i: (i, 0)),
                   pl.BlockSpec((None, MLA_HEADS * MLA_V, t), lambda i: (i // sb, 0, i % sb))],
        out_shape=[jax.ShapeDtypeStruct((n, MLA_HEADS * MLA_HEAD_LANES), BF16),
                   jax.ShapeDtypeStruct((n, MLA_HEADS * MLA_NOPE + LANES), BF16),
                   jax.ShapeDtypeStruct((batch, MLA_HEADS * MLA_V, seq), BF16)],
        compiler_params=_params("parallel"),
        name="mla_proj",
    )(h, gain, wd, qn, kvn, wuq, wuk, wuvt, cs)


def _mla_attn_kernel(q_ref, kn_ref, kr_ref, vt_ref, o_ref, kcat_ref):
    hp = ATTN_STEP_HEADS
    for h in range(hp):
        kcat_ref[h, :, :LANES] = kn_ref[:, h * LANES:(h + 1) * LANES]
        kcat_ref[h, :, LANES:] = kr_ref[...]
    seq = q_ref.shape[0]
    tq = ATTN_Q_TILE
    work = [(h, t) for h in range(hp) for t in range(seq // tq)]

    def scores(h, t):
        return _dot_nt(kcat_ref[h], q_ref[t * tq:(t + 1) * tq, h * MLA_HEAD_LANES:(h + 1) * MLA_HEAD_LANES])

    def attend(h, st):
        pt = jnp.exp2(st - jnp.max(st, axis=0, keepdims=True))
        l = jnp.sum(pt, axis=0, keepdims=True)
        ot = _dot(vt_ref[h * MLA_V:(h + 1) * MLA_V, :], pt.astype(BF16)) / l
        return ot.T.astype(o_ref.dtype)

    ahead = ATTN_LOOKAHEAD
    pending = [scores(*w) for w in work[:ahead]]
    outs = {}
    for n, (h, t) in enumerate(work):
        if n + ahead < len(work):
            pending.append(scores(*work[n + ahead]))
        outs[h, t] = attend(h, pending.pop(0))
    o_ref[...] = jnp.concatenate(
        [jnp.concatenate([outs[h, t] for t in range(seq // tq)], axis=0) for h in range(hp)], axis=1)


def _mla_attn(q, k, vt, batch, seq):
    n = q.shape[0]
    hp = ATTN_STEP_HEADS
    return pl.pallas_call(
        _mla_attn_kernel,
        grid=(batch, MLA_HEADS // hp),
        in_specs=[pl.BlockSpec((seq, hp * MLA_HEAD_LANES), lambda b, h: (b, h)),
                  pl.BlockSpec((seq, hp * LANES), lambda b, h: (b, h)),
                  pl.BlockSpec((seq, LANES), lambda b, h: (b, MLA_HEADS)),
                  pl.BlockSpec((None, hp * MLA_V, seq), lambda b, h: (b, h, 0))],
        out_specs=pl.BlockSpec((seq, hp * MLA_V), lambda b, h: (b, h)),
        out_shape=jax.ShapeDtypeStruct((n, MLA_HEADS * MLA_V), BF16),
        scratch_shapes=[pltpu.VMEM((hp, seq, MLA_HEAD_LANES), BF16)],
        compiler_params=_params("parallel", "arbitrary"),
        name="mla_attn",
    )(q, k, k, vt)


def _ffn_kernel(a_ref, wo_ref, h_ref, g_ref, wg_ref, wu_ref, wd_ref, o_ref):
    h1 = h_ref[...] + _dot(a_ref[...], wo_ref[...])
    xn = _rms(h1, g_ref[...]).astype(BF16)
    gate = _dot(xn, wg_ref[...])
    up = _dot(xn, wu_ref[...])
    act = (gate * jax.nn.sigmoid(gate) * up).astype(BF16)
    o_ref[...] = h1 + _dot(act, wd_ref[...])


def _resident(a):
    return pl.BlockSpec(a.shape, lambda i: (0,) * a.ndim, pipeline_mode=pl.Buffered(1))


def _ffn(a, wo, h, gain, wg, wu, wd):
    n = h.shape[0]
    t = ROW_TILE
    return pl.pallas_call(
        _ffn_kernel,
        grid=(n // t,),
        in_specs=[pl.BlockSpec((t, a.shape[1]), lambda i: (i, 0)), _resident(wo),
                  pl.BlockSpec((t, D_MODEL), lambda i: (i, 0)), _resident(gain),
                  _resident(wg), _resident(wu), _resident(wd)],
        out_specs=pl.BlockSpec((t, D_MODEL), lambda i: (i, 0)),
        out_shape=jax.ShapeDtypeStruct((n, D_MODEL), F32),
        compiler_params=_params("parallel"),
        name="ffn",
    )(a, wo, h, gain, wg, wu, wd)


def _norm_matmul_kernel(h_ref, g_ref, w_ref, o_ref):
    o_ref[...] = _dot(_rms(h_ref[...], g_ref[...]).astype(BF16), w_ref[...]).astype(o_ref.dtype)


def _norm_matmul(h, gain, w):
    n = h.shape[0]
    t = 2 * ROW_TILE
    return pl.pallas_call(
        _norm_matmul_kernel,
        grid=(n // t,),
        in_specs=[pl.BlockSpec((t, D_MODEL), lambda i: (i, 0)),
                  pl.BlockSpec(gain.shape, lambda i: (0, 0)),
                  pl.BlockSpec(w.shape, lambda i: (0, 0))],
        out_specs=pl.BlockSpec((t, w.shape[1]), lambda i: (i, 0)),
        out_shape=jax.ShapeDtypeStruct((n, w.shape[1]), BF16),
        compiler_params=_params("parallel"),
        name="norm_matmul",
    )(h, gain, w)


def _alibi_slope(head):
    return 2.0 ** (-8.0 * (head + 1) / SWA_Q_HEADS)


def _swa_bias_table():
    t = jnp.arange(SWA_BLOCK, dtype=I32)[:, None]
    s = jnp.arange(SWA_BLOCK, dtype=I32)[None, :]
    slots = []
    for j in (-1, 0, 1):
        dist = jnp.abs(t - (s + j * SWA_BLOCK))
        slots.append(jnp.where(dist <= SWA_WINDOW, -dist.astype(F32), NEG_INF))
    slots.append(jnp.full((SWA_BLOCK, SWA_BLOCK), NEG_INF, F32))
    base = jnp.stack(slots)
    slopes = jnp.asarray([_alibi_slope(h) * LOG2_E for h in range(SWA_Q_HEADS)], F32)
    per_head = slopes[:, None, None, None] * base[None]
    return jnp.concatenate([per_head[0::2], per_head[1::2]], axis=-1)


def _swa_attn_kernel(sink_ref, q_ref, k_ref, v_ref, bias_ref, o_ref, klo_ref, khi_ref, vlo_ref, vhi_ref):
    blk = SWA_BLOCK
    nb = k_ref.shape[0] // blk
    lane = lax.broadcasted_iota(I32, (blk, LANES), 1)
    lo = lane < SWA_HEAD_DIM

    @pl.when(pl.program_id(1) == 0)
    def _():
        keep = lax.broadcasted_iota(I32, k_ref.shape, 1) % LANES < SWA_HEAD_DIM
        klo_ref[...] = jnp.where(keep, k_ref[...], 0)
        khi_ref[...] = jnp.where(keep, 0, k_ref[...])
        vlo_ref[...] = jnp.where(keep, v_ref[...], 0)
        vhi_ref[...] = jnp.where(keep, 0, v_ref[...])

    def block(jb, carry):
        i = pl.program_id(1) * SWA_STEP_BLOCKS + jb
        q0 = pl.multiple_of(jb * blk, blk)
        rows, slots = [], []
        for j in (-1, 0, 1):
            kb = i + j
            rows.append(pl.multiple_of(jnp.clip(kb, 0, nb - 1) * blk, blk))
            slots.append(jnp.where((kb >= 0) & (kb < nb), j + 1, 3))

        def split(lo_ref, hi_ref, g, r0):
            return jnp.concatenate([lo_ref[pl.ds(r0, blk), g * LANES:(g + 1) * LANES],
                                    hi_ref[pl.ds(r0, blk), g * LANES:(g + 1) * LANES]], axis=0)

        def scores(pair):
            g = 2 * pair // SWA_GROUP
            qp = q_ref[pl.ds(q0, blk), pair * LANES:(pair + 1) * LANES]
            return [_dot_nt(qp, split(klo_ref, khi_ref, g, rows[j])) + bias_ref[pair, slots[j]] for j in range(3)]

        def attend(pair, ss):
            h0 = 2 * pair
            ps, ls = [], []
            for half, head in ((0, h0), (1, h0 + 1)):
                sl = slice(half * blk, (half + 1) * blk)
                snk = sink_ref[head] * LOG2_E
                m = jnp.max(jnp.maximum(jnp.maximum(ss[0][:, sl], ss[1][:, sl]), ss[2][:, sl]),
                            axis=-1, keepdims=True)
                m = jnp.maximum(m, snk)
                p3 = [jnp.exp2(ss[j][:, sl] - m) for j in range(3)]
                ls.append(jnp.sum(p3[0] + p3[1] + p3[2], axis=-1, keepdims=True) + jnp.exp2(snk - m))
                ps.append(p3)
            o = jnp.zeros((blk, LANES), F32)
            for j in range(3):
                pj = jnp.concatenate([ps[0][j], ps[1][j]], axis=1).astype(BF16)
                o = o + _dot(pj, split(vlo_ref, vhi_ref, h0 // SWA_GROUP, rows[j]))
            return (o / jnp.where(lo, ls[0], ls[1])).astype(o_ref.dtype)

        n_pairs = SWA_Q_HEADS // 2
        pending = {p: scores(p) for p in range(min(SWA_LOOKAHEAD, n_pairs))}
        outs = []
        for p in range(n_pairs):
            if p + SWA_LOOKAHEAD < n_pairs:
                pending[p + SWA_LOOKAHEAD] = scores(p + SWA_LOOKAHEAD)
            outs.append(attend(p, pending.pop(p)))
        o_ref[pl.ds(q0, blk), :] = jnp.concatenate(outs, axis=1)
        return carry

    lax.fori_loop(0, SWA_STEP_BLOCKS, block, 0)


def _swa_attn(qkv, sink, bias, batch, seq):
    n = qkv.shape[0]
    rows = SWA_STEP_BLOCKS * SWA_BLOCK
    steps = seq // rows
    qw = SWA_Q_HEADS * SWA_HEAD_DIM
    kvw = SWA_KV_HEADS * LANES
    kblk = qw // kvw
    return pl.pallas_call(
        _swa_attn_kernel,
        grid=(batch, steps),
        in_specs=[pl.BlockSpec(memory_space=pltpu.SMEM),
                  pl.BlockSpec((rows, qw), lambda b, i: (b * steps + i, 0)),
                  pl.BlockSpec((seq, kvw), lambda b, i: (b, kblk)),
                  pl.BlockSpec((seq, kvw), lambda b, i: (b, kblk + 1)),
                  pl.BlockSpec(bias.shape, lambda b, i: (0, 0, 0, 0), pipeline_mode=pl.Buffered(1))],
        out_specs=pl.BlockSpec((rows, qw), lambda b, i: (b * steps + i, 0)),
        out_shape=jax.ShapeDtypeStruct((n, qw), BF16),
        scratch_shapes=[pltpu.VMEM((seq, kvw), BF16) for _ in range(4)],
        compiler_params=_params("arbitrary", "arbitrary"),
        name="swa_attn",
    )(sink, qkv, qkv, qkv, bias)


def _moe_route_kernel(a_ref, wo_ref, h_ref, g_ref, wr_ref, h1_ref, xs_ref, pos_ref, gate_ref, cnt_ref):
    t = MOE_CHUNK
    h1 = h_ref[...] + _dot(a_ref[...], wo_ref[...])
    h1_ref[...] = h1
    xn = _rms(h1, g_ref[...])
    x_hi = xn.astype(BF16)
    x_lo = (xn - x_hi.astype(F32)).astype(BF16)
    hi = _dot(x_hi, wr_ref[...])
    logits = hi[:, :LANES] + hi[:, LANES:] + _dot(x_lo, wr_ref[:, :LANES])
    lt = logits.T[:N_EXPERTS]
    eio = lax.broadcasted_iota(I32, lt.shape, 0)
    m1 = jnp.max(lt, axis=0, keepdims=True)
    i1 = jnp.min(jnp.where(lt == m1, eio, N_EXPERTS), axis=0, keepdims=True)
    l2 = jnp.where(eio == i1, -jnp.inf, lt)
    m2 = jnp.max(l2, axis=0, keepdims=True)
    i2 = jnp.min(jnp.where(l2 == m2, eio, N_EXPERTS), axis=0, keepdims=True)
    e21 = jnp.exp(m2 - m1)
    g1 = 1.0 / (1.0 + e21)
    g2 = e21 * g1

    sel1 = eio == i1
    sel2 = eio == i2
    oh = jnp.where(sel1 | sel2, 1.0, 0.0)
    tr = lax.broadcasted_iota(I32, (t, t), 0)
    tc = lax.broadcasted_iota(I32, (t, t), 1)
    before = jnp.where(tr < tc, 1.0, 0.0).astype(BF16)
    rank = _dot(oh.astype(BF16), before)
    cnt = jnp.sum(oh, axis=1, keepdims=True).astype(I32)
    padded = jnp.bitwise_and(cnt + (MOE_GRANULE - 1), -MOE_GRANULE)
    offs, run = [], jnp.zeros((1, 1), I32)
    for e in range(N_EXPERTS):
        offs.append(run)
        run = run + padded[e:e + 1, :]
    off = jnp.concatenate(offs, axis=0).astype(F32)
    slot = rank + off
    pos1 = jnp.sum(jnp.where(sel1, slot, 0.0), axis=0, keepdims=True).astype(I32)
    pos2 = jnp.sum(jnp.where(sel2, slot, 0.0), axis=0, keepdims=True).astype(I32)

    rio = lax.broadcasted_iota(I32, (MOE_CHUNK_ROWS, t), 0)
    perm = jnp.where(rio == pos1, 1.0, jnp.where(rio == pos2, 1.0, 0.0)).astype(BF16)
    xs_ref[...] = _dot(perm, x_hi).astype(BF16)

    zi = jnp.zeros((N_EXPERTS - 2, t), I32)
    pos_ref[...] = jnp.concatenate([pos1, pos2, zi], axis=0)
    gate_ref[...] = jnp.concatenate([g1, g2, zi.astype(F32)], axis=0)
    cnt_ref[...] = jnp.broadcast_to(padded, (N_EXPERTS, LANES))


def _moe_route(a, wo, h, gain, wr):
    n = h.shape[0]
    t = MOE_CHUNK
    nc = n // t
    return pl.pallas_call(
        _moe_route_kernel,
        grid=(nc,),
        in_specs=[pl.BlockSpec((t, a.shape[1]), lambda i: (i, 0)),
                  pl.BlockSpec(wo.shape, lambda i: (0, 0)),
                  pl.BlockSpec((t, D_MODEL), lambda i: (i, 0)),
                  pl.BlockSpec(gain.shape, lambda i: (0, 0)),
                  pl.BlockSpec(wr.shape, lambda i: (0, 0))],
        out_specs=[pl.BlockSpec((t, D_MODEL), lambda i: (i, 0)),
                   pl.BlockSpec((MOE_CHUNK_ROWS, D_MODEL), lambda i: (i, 0)),
                   pl.BlockSpec((None, N_EXPERTS, t), lambda i: (i, 0, 0)),
                   pl.BlockSpec((None, N_EXPERTS, t), lambda i: (i, 0, 0)),
                   pl.BlockSpec((None, N_EXPERTS, LANES), lambda i: (i, 0, 0))],
        out_shape=[jax.ShapeDtypeStruct((n, D_MODEL), F32),
                   jax.ShapeDtypeStruct((nc * MOE_CHUNK_ROWS, D_MODEL), BF16),
                   jax.ShapeDtypeStruct((nc, N_EXPERTS, t), I32),
                   jax.ShapeDtypeStruct((nc, N_EXPERTS, t), F32),
                   jax.ShapeDtypeStruct((nc, N_EXPERTS, LANES), I32)],
        compiler_params=_params("parallel"),
        name="moe_route",
    )(a, wo, h, gain, wr)


def _moe_plan(padded, n_tokens):
    nc = padded.shape[0]
    g = MOE_GRANULE
    loc_off = jnp.cumsum(padded, axis=1) - padded
    total = jnp.sum(padded, axis=0)
    total_pad = ((total + MOE_ROW_TILE - 1) // MOE_ROW_TILE) * MOE_ROW_TILE
    gbase = jnp.cumsum(total_pad) - total_pad
    coff = jnp.cumsum(padded, axis=0) - padded
    shift = gbase[None, :] + coff - loc_off
    step = shift[:, 1:] - shift[:, :-1]
    ends = (loc_off + padded)[:, :-1]
    row = jnp.arange(MOE_GRANULES, dtype=I32)[None, :, None] * g
    dest = row[:, :, 0] + shift[:, :1] + jnp.sum(jnp.where(ends[:, None, :] <= row, step[:, None, :], 0), axis=2)
    n_used = jnp.sum(padded, axis=1) // g
    valid = jnp.arange(MOE_GRANULES, dtype=I32)[None, :] < n_used[:, None]
    gdest = jnp.where(valid, dest // g, 0).astype(I32).reshape(-1)
    n_tiles = _moe_tiles(n_tokens)
    tile_row = jnp.arange(n_tiles, dtype=I32) * MOE_ROW_TILE
    tile_expert = jnp.minimum(jnp.sum(((gbase + total_pad)[None, :] <= tile_row[:, None]).astype(I32), axis=1),
                              N_EXPERTS - 1).astype(I32)
    tiles_used = ((gbase[-1] + total_pad[-1]) // MOE_ROW_TILE).astype(I32).reshape(1)
    back = (jnp.arange(nc, dtype=I32)[:, None] * MOE_CHUNK_ROWS + loc_off - gbase[None, :] - coff) // g
    run_end = (gbase[None, :] + coff + padded) // g
    gran = jnp.arange(n_tiles * MOE_TILE_GRANULES, dtype=I32)
    own = (gbase[None, :] <= gran[:, None] * g) & (gran[:, None] * g < (gbase + total_pad)[None, :])
    pick = lambda tbl: jnp.sum(jnp.where(own[:, None, :], tbl[None, :, :], 0), axis=2)
    hops = jnp.sum(jnp.where(pick(run_end[:-1]) <= gran[:, None], pick(back[1:] - back[:-1]), 0), axis=1)
    src = gran + pick(back[:1])[:, 0] + hops
    filled = gran < pick(((gbase + total) // g)[None, :])[:, 0]
    src = jnp.where(filled, src, MOE_GRANULES - 1).astype(I32)
    return gdest, tile_expert, tiles_used, src


def _moe_tiles(n_tokens):
    rows = (n_tokens // MOE_CHUNK) * MOE_CHUNK_ROWS + N_EXPERTS * MOE_ROW_TILE
    return -(-rows // MOE_ROW_TILE)


def _granule_copy(src, dst, sem):
    return pltpu.make_async_copy(src, dst, sem)


def _moe_expert_kernel(src_ref, te_ref, nt_ref, xs_ref, wg_ref, wu_ref, wd_ref, o_ref, xbuf_ref, sem):
    del te_ref
    i = pl.program_id(0)
    n = nt_ref[0]
    g = MOE_GRANULE

    def copy(tile, k):
        s = src_ref[tile * MOE_TILE_GRANULES + k]
        return _granule_copy(xs_ref.at[pl.ds(pl.multiple_of(s * g, g), g)],
                             xbuf_ref.at[tile % 2, pl.ds(pl.multiple_of(k * g, g), g)], sem.at[tile % 2])

    def for_tile(tile, action):
        def step(k, carry):
            action(copy(tile, k))
            return carry
        lax.fori_loop(0, MOE_TILE_GRANULES, step, 0, unroll=MOE_DMA_UNROLL)

    @pl.when(i == 0)
    def _():
        for_tile(i, lambda cp: cp.start())

    @pl.when(i + 1 < n)
    def _():
        for_tile(i + 1, lambda cp: cp.start())

    @pl.when(i < n)
    def _():
        for_tile(i, lambda cp: cp.wait())
        x = xbuf_ref[i % 2]
        gate = _dot(x, wg_ref[...])
        up = _dot(x, wu_ref[...])
        act = (gate * jax.nn.sigmoid(gate) * up).astype(BF16)
        o_ref[...] = _dot(act, wd_ref[...]).astype(o_ref.dtype)

    @pl.when(i >= n)
    def _():
        o_ref[...] = jnp.zeros_like(o_ref)


def _moe_expert(src, tile_expert, tiles_used, xs, wg, wu, wd):
    nt = tile_expert.shape[0]

    def expert(i, src_ref, te_ref, nt_ref):
        return te_ref[jnp.minimum(i, nt_ref[0] - 1)]

    return pl.pallas_call(
        _moe_expert_kernel,
        grid_spec=pltpu.PrefetchScalarGridSpec(
            num_scalar_prefetch=3,
            grid=(nt,),
            in_specs=[pl.BlockSpec(memory_space=pl.ANY),
                      pl.BlockSpec((None, D_MODEL, EXPERT_DIM), lambda i, *p: (expert(i, *p), 0, 0)),
                      pl.BlockSpec((None, D_MODEL, EXPERT_DIM), lambda i, *p: (expert(i, *p), 0, 0)),
                      pl.BlockSpec((None, EXPERT_DIM, D_MODEL), lambda i, *p: (expert(i, *p), 0, 0))],
            out_specs=pl.BlockSpec((MOE_ROW_TILE, D_MODEL), lambda i, *p: (i, 0)),
            scratch_shapes=[pltpu.VMEM((2, MOE_ROW_TILE, D_MODEL), BF16), pltpu.SemaphoreType.DMA((2,))]),
        out_shape=jax.ShapeDtypeStruct((nt * MOE_ROW_TILE, D_MODEL), BF16),
        compiler_params=_params("arbitrary"),
        name="moe_expert",
    )(src, tile_expert, tiles_used, xs, wg, wu, wd)


def _moe_combine_kernel(gd_ref, h_ref, pos_ref, gate_ref, fg_ref, ys_ref, o_ref, ybuf_ref, sem,
                        *, final_norm):
    c = pl.program_id(0)
    nc = pl.num_programs(0)
    g = MOE_GRANULE
    t = MOE_CHUNK
    slot = c % 2

    def copy(chunk, j):
        d = gd_ref[chunk * MOE_GRANULES + j]
        return _granule_copy(ys_ref.at[pl.ds(pl.multiple_of(d * g, g), g)],
                             ybuf_ref.at[chunk % 2, pl.ds(pl.multiple_of(j * g, g), g)], sem.at[chunk % 2])

    def for_chunk(chunk, action):
        def step(j, carry):
            action(copy(chunk, j))
            return carry
        lax.fori_loop(0, MOE_GRANULES, step, 0, unroll=MOE_DMA_UNROLL)

    @pl.when(c == 0)
    def _():
        for_chunk(c, lambda cp: cp.start())

    @pl.when(c + 1 < nc)
    def _():
        for_chunk(c + 1, lambda cp: cp.start())

    for_chunk(c, lambda cp: cp.wait())

    rio = lax.broadcasted_iota(I32, (MOE_CHUNK_ROWS, t), 0)
    w = (jnp.where(rio == pos_ref[0:1, :], gate_ref[0:1, :], 0.0)
         + jnp.where(rio == pos_ref[1:2, :], gate_ref[1:2, :], 0.0)).astype(BF16)
    y = lax.dot_general(w, ybuf_ref[slot], (((0,), (0,)), ((), ())), preferred_element_type=F32)
    out = h_ref[...] + y
    if final_norm:
        out = _rms(out, fg_ref[...])
    o_ref[...] = out


def _moe_combine(gdest, h, pos, gates, ys, final_gain, final_norm):
    n = h.shape[0]
    t = MOE_CHUNK
    nc = n // t
    return pl.pallas_call(
        functools.partial(_moe_combine_kernel, final_norm=final_norm),
        grid_spec=pltpu.PrefetchScalarGridSpec(
            num_scalar_prefetch=1,
            grid=(nc,),
            in_specs=[pl.BlockSpec((t, D_MODEL), lambda i, gd: (i, 0)),
                      pl.BlockSpec((None, N_EXPERTS, t), lambda i, gd: (i, 0, 0)),
                      pl.BlockSpec((None, N_EXPERTS, t), lambda i, gd: (i, 0, 0)),
                      pl.BlockSpec(final_gain.shape, lambda i, gd: (0, 0)),
                      pl.BlockSpec(memory_space=pl.ANY)],
            out_specs=pl.BlockSpec((t, D_MODEL), lambda i, gd: (i, 0)),
            scratch_shapes=[pltpu.VMEM((2, MOE_CHUNK_ROWS, D_MODEL), BF16), pltpu.SemaphoreType.DMA((2,))]),
        out_shape=jax.ShapeDtypeStruct((n, D_MODEL), F32),
        compiler_params=_params("arbitrary"),
        name="moe_combine",
    )(gdest, h, pos, gates, final_gain, ys)


def _moe(a, wo, h, gain, router, wg, wu, wd, final_gain, final_norm):
    n = h.shape[0]
    wr = jnp.pad(router, ((0, 0), (0, LANES - N_EXPERTS)))
    wr_hi = wr.astype(BF16)
    wr_lo = (wr - wr_hi.astype(F32)).astype(BF16)
    h, xs, pos, gates, cnt = _moe_route(a, wo, h, gain, jnp.concatenate([wr_hi, wr_lo], axis=1))
    gdest, tile_expert, tiles_used, src = _moe_plan(cnt[:, :, 0], n)
    ys = _moe_expert(src, tile_expert, tiles_used, xs, wg.astype(BF16), wu.astype(BF16), wd.astype(BF16))
    return _moe_combine(gdest, h, pos, gates, ys, final_gain, final_norm)


def _rot_cols(w):
    half = MLA_ROPE // 2
    return jnp.concatenate([-w[..., half:], w[..., :half]], axis=-1)


def _rope_table(seq):
    inv = 1.0 / (ROPE_THETA ** (jnp.arange(0, MLA_ROPE, 2, dtype=F32) / MLA_ROPE))
    ang = jnp.arange(seq, dtype=F32)[:, None] * inv[None, :]
    cos, sin = jnp.cos(ang), jnp.sin(ang)
    return jnp.concatenate([cos, cos, sin, sin], axis=-1)


def _mla_weights(w_dqkv, w_uq, w_uk, w_uv):
    lat = MLA_Q_RANK + MLA_KV_RANK
    rope = w_dqkv[:, lat:]
    wd = jnp.concatenate([w_dqkv[:, :lat], rope, _rot_cols(rope)], axis=1).astype(BF16)
    q_rope = w_uq[..., MLA_NOPE:]
    wuq = jnp.concatenate([w_uq[..., :MLA_NOPE], q_rope, _rot_cols(q_rope)], axis=-1)
    wuq = wuq.reshape(MLA_Q_RANK, MLA_HEADS * MLA_HEAD_LANES).astype(BF16)
    wuk = w_uk.reshape(MLA_KV_RANK, -1).astype(BF16)
    wuvt = w_uv.reshape(MLA_KV_RANK, -1).T.astype(BF16)
    return wd, wuq, wuk, wuvt


def _swa_weights(w_qkv):
    qw = SWA_Q_HEADS * SWA_HEAD_DIM
    kvw = SWA_KV_HEADS * SWA_HEAD_DIM
    dup = lambda w: jnp.concatenate([w.reshape(D_MODEL, SWA_KV_HEADS, 1, SWA_HEAD_DIM)] * 2, axis=2).reshape(D_MODEL, -1)
    wq = w_qkv[:, :qw] * (SWA_HEAD_DIM ** -0.5 * LOG2_E)
    return jnp.concatenate([wq, dup(w_qkv[:, qw:qw + kvw]), dup(w_qkv[:, qw + kvw:])], axis=1).astype(BF16)


def kernel(x, mla_norm, mla_w_dqkv, mla_q_norm, mla_w_uq, mla_kv_norm, mla_w_uk, mla_w_uv, mla_w_o, swa_norm, swa_w_qkv, swa_sink, swa_w_o, ffn_norm, ffn_w_gate, ffn_w_up, ffn_w_down, moe_norm, moe_router, moe_w_gate, moe_w_up, moe_w_down, final_norm):
    batch, seq, _ = x.shape
    h = x.reshape(batch * seq, D_MODEL)
    cs = _rope_table(seq)
    swa_bias = _swa_bias_table()
    row = lambda v: v.reshape(1, -1).astype(F32)
    depth = 2 * mla_norm.shape[0]
    for layer in range(depth):
        j = layer // 2
        if layer % 2 == 0:
            wd, wuq, wuk, wuvt = _mla_weights(mla_w_dqkv[j], mla_w_uq[j], mla_w_uk[j], mla_w_uv[j])
            q, k, vt = _mla_proj(h, batch, seq, row(mla_norm[j]), wd, row(mla_q_norm[j]), row(mla_kv_norm[j]),
                                 wuq, wuk, wuvt, cs)
            a = _mla_attn(q, k, vt, batch, seq)
            h = _ffn(a, mla_w_o[j].astype(BF16), h, row(ffn_norm[j]), ffn_w_gate[j].astype(BF16),
                     ffn_w_up[j].astype(BF16), ffn_w_down[j].astype(BF16))
        else:
            qkv = _norm_matmul(h, row(swa_norm[j]), _swa_weights(swa_w_qkv[j]))
            a = _swa_attn(qkv, swa_sink[j].astype(F32), swa_bias, batch, seq)
            h = _moe(a, swa_w_o[j].astype(BF16), h, row(moe_norm[j]), moe_router[j], moe_w_gate[j], moe_w_up[j],
                     moe_w_down[j], row(final_norm), layer == depth - 1)
    return h.reshape(batch, seq, D_MODEL)
```
